```python
import jax, jax.numpy as jnp
from jax import lax
import numpy as np

D_MODEL = 2048
BATCH = 8
SEQ = 4096
DEPTH = 1

HEAD_DIM = 64
N_Q_HEADS = 16
N_KV_HEADS = 4
ATTN_WIDTH = N_Q_HEADS * HEAD_DIM
KV_WIDTH = N_KV_HEADS * HEAD_DIM
CONV_WIDTH = D_MODEL - ATTN_WIDTH
CONV_GROUPS = 16
CONV_GROUP_DIM = CONV_WIDTH // CONV_GROUPS
CONV_K = 3
IN_COLS = ATTN_WIDTH + 2 * KV_WIDTH + 3 * CONV_WIDTH

WINDOW = 128
ROPE_THETA = 500000.0
ROPE_DIM = HEAD_DIM // 4
ATTN_SCALE = HEAD_DIM ** -0.5
NEG_INF = -1e30

PEER_HEADS = 8
PEER_NKEYS = 128
PEER_N = PEER_NKEYS * PEER_NKEYS
PEER_DKEY = 256
PEER_HALF = PEER_DKEY // 2
PEER_TOPK = 16
PEER_CHUNK = 128

EPS = 1e-6

kernel_name = "hymba_swa_shortconv_peer_adaln"


def rmsnorm(x, gain):
    xf = x.astype(jnp.float32)
    y = xf * lax.rsqrt(jnp.mean(xf * xf, axis=-1, keepdims=True) + EPS)
    return (y * gain.astype(jnp.float32)).astype(x.dtype)


def head_rmsnorm(y, gain, n_heads):
    b, s, w = y.shape
    yf = y.reshape(b, s, n_heads, w // n_heads).astype(jnp.float32)
    yf = yf * lax.rsqrt(jnp.mean(yf * yf, axis=-1, keepdims=True) + EPS)
    return (yf.reshape(b, s, w) * gain.astype(jnp.float32)).astype(y.dtype)


def partial_rope(t, cos, sin):
    half = ROPE_DIM // 2
    t1 = t[..., :half]
    t2 = t[..., half:ROPE_DIM]
    rot = jnp.concatenate([t1 * cos - t2 * sin, t2 * cos + t1 * sin], axis=-1)
    return jnp.concatenate([rot.astype(t.dtype), t[..., ROPE_DIM:]], axis=-1)


def sliding_window_attention(q, k, v, sinks):
    b, s = q.shape[:2]
    nb = s // WINDOW
    g = N_Q_HEADS // N_KV_HEADS
    qb = q.reshape(b, nb, WINDOW, N_KV_HEADS, g, HEAD_DIM).transpose(1, 0, 2, 3, 4, 5)

    def band(t):
        tb = t.reshape(b, nb, WINDOW, N_KV_HEADS, HEAD_DIM)
        prev = jnp.concatenate([jnp.zeros_like(tb[:, :1]), tb[:, :-1]], axis=1)
        return jnp.concatenate([prev, tb], axis=2).transpose(1, 0, 2, 3, 4)

    kb, vb = band(k), band(v)
    sink = sinks.astype(jnp.float32).reshape(1, N_KV_HEADS, g, 1, 1)
    qi = jnp.arange(WINDOW)[:, None]
    kj = jnp.arange(2 * WINDOW)[None, :]
    diff = WINDOW + qi - kj
    band_mask = (diff >= 0) & (diff < WINDOW)

    def one_block(args):
        qn, kn, vn, n = args
        sc = jnp.einsum('bqhgd,bkhd->bhgqk', qn, kn,
                        preferred_element_type=jnp.float32) * ATTN_SCALE
        key_pos = (n - 1) * WINDOW + kj
        mask = band_mask & (key_pos >= 0)
        sc = jnp.where(mask, sc, NEG_INF)
        m = jnp.maximum(jnp.max(sc, axis=-1, keepdims=True), sink)
        p = jnp.exp(sc - m)
        denom = jnp.sum(p, axis=-1, keepdims=True) + jnp.exp(sink - m)
        probs = (p / denom).astype(vn.dtype)
        return jnp.einsum('bhgqk,bkhd->bqhgd', probs, vn)

    out = lax.map(one_block, (qb, kb, vb, jnp.arange(nb)))
    return out.transpose(1, 0, 2, 3, 4, 5).reshape(b, s, ATTN_WIDTH)


def short_conv(bg, cg, hc, conv_w):
    s = hc.shape[1]
    u = cg * hc
    upad = jnp.pad(u, ((0, 0), (CONV_K - 1, 0), (0, 0)))
    acc = conv_w[0] * upad[:, 0:s]
    for tap in range(1, CONV_K):
        acc = acc + conv_w[tap] * upad[:, tap:tap + s]
    return bg * acc


def peer(h, w_pq, peer_subkeys, peer_u, peer_v):
    b, s, d = h.shape
    q = jnp.einsum('bsd,de->bse', h, w_pq).reshape(b, s, PEER_HEADS, PEER_DKEY)
    q1, q2 = q[..., :PEER_HALF], q[..., PEER_HALF:]
    s1 = jnp.einsum('bshd,hkd->bshk', q1, peer_subkeys[:, 0], preferred_element_type=jnp.float32)
    s2 = jnp.einsum('bshd,hkd->bshk', q2, peer_subkeys[:, 1], preferred_element_type=jnp.float32)
    t1, i1 = lax.top_k(s1, PEER_TOPK)
    t2, i2 = lax.top_k(s2, PEER_TOPK)
    cand = (t1[..., :, None] + t2[..., None, :]).reshape(b, s, PEER_HEADS, PEER_TOPK * PEER_TOPK)
    ts, ci = lax.top_k(cand, PEER_TOPK)
    e1 = jnp.take_along_axis(i1, ci // PEER_TOPK, axis=-1)
    e2 = jnp.take_along_axis(i2, ci % PEER_TOPK, axis=-1)
    expert = e1 * PEER_NKEYS + e2
    gates = jax.nn.softmax(ts, axis=-1).astype(h.dtype)

    n_tok = b * s
    n_chunk = n_tok // PEER_CHUNK
    xt = h.reshape(n_chunk, PEER_CHUNK, d)
    idx = expert.reshape(n_chunk, PEER_CHUNK, PEER_HEADS * PEER_TOPK)
    gw = gates.reshape(n_chunk, PEER_CHUNK, PEER_HEADS * PEER_TOPK)

    def chunk_fn(args):
        xc, ic, gc = args
        u = peer_u[ic]
        a = jax.nn.gelu(jnp.einsum('cd,ced->ce', xc, u), approximate=False)
        vv = peer_v[ic]
        return jnp.einsum('ce,ced->cd', gc * a, vv)

    y = lax.map(chunk_fn, (xt, idx, gw))
    return y.reshape(b, s, d)


def setup_inputs(seed: int = 0) -> dict:
    key = jax.random.key(seed)
    ks = jax.random.split(key, 20)
    f32 = jnp.float32
    nrm = lambda k, shape, scale: jax.random.normal(k, shape, f32) * scale
    return {
        "x": nrm(ks[0], (BATCH, SEQ, D_MODEL), 1.0),
        "c": nrm(ks[1], (BATCH, D_MODEL), 1.0),
        "w_ada": nrm(ks[2], (D_MODEL, 6 * D_MODEL), 0.5 * D_MODEL ** -0.5),
        "b_ada": nrm(ks[3], (6 * D_MODEL,), 0.01),
        "g_norm1": 1.0 + nrm(ks[4], (D_MODEL,), 0.02),
        "w_in": nrm(ks[5], (D_MODEL, IN_COLS), D_MODEL ** -0.5),
        "g_q": 1.0 + nrm(ks[6], (HEAD_DIM,), 0.02),
        "g_k": 1.0 + nrm(ks[7], (HEAD_DIM,), 0.02),
        "sinks": nrm(ks[8], (N_Q_HEADS,), 0.5),
        "conv_w": nrm(ks[9], (CONV_K, CONV_WIDTH), CONV_K ** -0.5),
        "g_out_attn": 1.0 + nrm(ks[10], (ATTN_WIDTH,), 0.02),
        "g_out_conv": 1.0 + nrm(ks[11], (CONV_WIDTH,), 0.02),
        "w_out": nrm(ks[12], (D_MODEL, D_MODEL), D_MODEL ** -0.5),
        "g_norm2": 1.0 + nrm(ks[13], (D_MODEL,), 0.02),
        "w_pq": nrm(ks[14], (D_MODEL, PEER_HEADS * PEER_DKEY), D_MODEL ** -0.5),
        "peer_subkeys": nrm(ks[15], (PEER_HEADS, 2, PEER_NKEYS, PEER_HALF), PEER_HALF ** -0.5),
        "peer_u": nrm(ks[16], (PEER_N, D_MODEL), D_MODEL ** -0.5),
        "peer_v": nrm(ks[17], (PEER_N, D_MODEL), 0.5),
    }


def reference(x, c, w_ada, b_ada, g_norm1, w_in, g_q, g_k, sinks, conv_w,
              g_out_attn, g_out_conv, w_out, g_norm2, w_pq, peer_subkeys,
              peer_u, peer_v):
    b, s, d = x.shape
    pos = jnp.arange(s, dtype=jnp.float32)
    inv_freq = ROPE_THETA ** (-jnp.arange(0, ROPE_DIM, 2, dtype=jnp.float32) / ROPE_DIM)
    ang = pos[:, None] * inv_freq[None, :]
    cos = jnp.cos(ang)[None, :, None, :].astype(x.dtype)
    sin = jnp.sin(ang)[None, :, None, :].astype(x.dtype)

    for _ in range(DEPTH):
        mod = jnp.einsum('bd,de->be', jax.nn.silu(c), w_ada) + b_ada
        sh1, sc1, gt1, sh2, sc2, gt2 = jnp.split(mod[:, None, :], 6, axis=-1)

        h1 = rmsnorm(x, g_norm1) * (1.0 + sc1) + sh1
        proj = jnp.einsum('bsd,de->bse', h1, w_in)
        o = 0
        q = proj[..., o:o + ATTN_WIDTH]; o += ATTN_WIDTH
        k = proj[..., o:o + KV_WIDTH]; o += KV_WIDTH
        v = proj[..., o:o + KV_WIDTH]; o += KV_WIDTH
        bg = proj[..., o:o + CONV_WIDTH]; o += CONV_WIDTH
        cg = proj[..., o:o + CONV_WIDTH]; o += CONV_WIDTH
        hc = proj[..., o:o + CONV_WIDTH]

        q = rmsnorm(q.reshape(b, s, N_Q_HEADS, HEAD_DIM), g_q)
        k = rmsnorm(k.reshape(b, s, N_KV_HEADS, HEAD_DIM), g_k)
        v = v.reshape(b, s, N_KV_HEADS, HEAD_DIM)
        q = partial_rope(q, cos, sin)
        k = partial_rope(k, cos, sin)
        attn = sliding_window_attention(q, k, v, sinks)
        conv = short_conv(bg, cg, hc, conv_w)

        mixed = jnp.concatenate([head_rmsnorm(attn, g_out_attn, N_Q_HEADS),
                                 head_rmsnorm(conv, g_out_conv, CONV_GROUPS)], axis=-1)
        x = x + gt1 * jnp.einsum('bse,ed->bsd', mixed, w_out)

        h2 = rmsnorm(x, g_norm2) * (1.0 + sc2) + sh2
        x = x + gt2 * peer(h2, w_pq, peer_subkeys, peer_u, peer_v)
    return x
```

```python
import functools
import math

import jax
import jax.numpy as jnp
from jax import lax
from jax.experimental import pallas as pl
from jax.experimental.pallas import tpu as pltpu

F32 = jnp.float32
BF16 = jnp.bfloat16

D_MODEL = 2048
HEAD_DIM = 64
N_Q_HEADS = 16
N_KV_HEADS = 4
Q_PER_KV = N_Q_HEADS // N_KV_HEADS
ATTN_WIDTH = N_Q_HEADS * HEAD_DIM
KV_WIDTH = N_KV_HEADS * HEAD_DIM
CONV_WIDTH = D_MODEL - ATTN_WIDTH
CONV_K = 3
IN_COLS = ATTN_WIDTH + 2 * KV_WIDTH + 3 * CONV_WIDTH
WINDOW = 128
ROPE_THETA = 500000.0
ROPE_DIM = HEAD_DIM // 4
ATTN_SCALE = HEAD_DIM ** -0.5
NEG_INF = -1e30
PEER_HEADS = 8
PEER_NKEYS = 128
PEER_N = PEER_NKEYS * PEER_NKEYS
PEER_HALF = 128
PEER_TOPK = 16
EPS = 1e-6

LANES = 128
MXU_DIM = 256
VMEM_LIMIT = 56 * 1024 * 1024

ADA_COLS = 1024
MIX_ROWS = 256
ROUTE_ROWS = 256
PEER_ROWS = 512
PEER_E1 = 8
E2_ROWS = 32

_PAIRS = tuple((a, b) for a in range(PEER_TOPK) for b in range(PEER_TOPK)
               if (a + 1) * (b + 1) <= PEER_TOPK)


def _dot_t(a, b):
    return lax.dot_general(a, b, (((1,), (1,)), ((), ())),
                           preferred_element_type=F32)


def _group_mean_sq(y, bd):
    y2 = y * y
    hi = y2.astype(BF16)
    lo = (y2 - hi.astype(F32)).astype(BF16)
    outs = []
    for j in range(y.shape[1] // MXU_DIM):
        sl = slice(MXU_DIM * j, MXU_DIM * (j + 1))
        outs.append(jnp.dot(hi[:, sl], bd, preferred_element_type=F32)
                    + jnp.dot(lo[:, sl], bd, preferred_element_type=F32))
    ss = outs[0] if len(outs) == 1 else jnp.concatenate(outs, axis=1)
    return ss * (1.0 / HEAD_DIM)


def _rope(t, rc, rs1, rs2):
    outs = []
    for j in range(t.shape[1] // LANES):
        tj = t[:, LANES * j:LANES * (j + 1)]
        outs.append(tj * rc
                    + pltpu.roll(tj, LANES - ROPE_DIM // 2, 1) * rs1
                    + pltpu.roll(tj, ROPE_DIM // 2, 1) * rs2)
    return outs[0] if len(outs) == 1 else jnp.concatenate(outs, axis=1)


def _ada_kernel(c_ref, w_ref, b_ref, o_ref):
    c = c_ref[...]
    s = c * jax.nn.sigmoid(c)
    o_ref[...] = jnp.dot(s, w_ref[...], preferred_element_type=F32,
                         precision=lax.Precision.HIGHEST) + b_ref[...]


def _ada(c, w_ada, b_ada):
    b, d = c.shape
    n = w_ada.shape[1]
    return pl.pallas_call(
        _ada_kernel,
        grid=(n // ADA_COLS,),
        in_specs=[pl.BlockSpec((b, d), lambda j: (0, 0)),
                  pl.BlockSpec((d, ADA_COLS), lambda j: (0, j)),
                  pl.BlockSpec((1, ADA_COLS), lambda j: (0, j))],
        out_specs=pl.BlockSpec((b, ADA_COLS), lambda j: (0, j)),
        out_shape=jax.ShapeDtypeStruct((b, n), F32),
        compiler_params=pltpu.CompilerParams(
            dimension_semantics=("arbitrary",), vmem_limit_bytes=VMEM_LIMIT),
        name="ada",
    )(c, w_ada, b_ada.reshape(1, n))


def _mixer_kernel(sinks_ref, x_ref, mod_ref, g1_ref, win_ref, gq_ref, gk_ref,
                  rc_ref, rs1_ref, rs2_ref, cw_ref, goa_ref, goc_ref, wout_ref,
                  bd_ref, o_ref, kk_ref, vv_ref, u_ref, qn_ref, mixed_ref):
    ts = x_ref.shape[0]
    si = pl.program_id(1)
    x = x_ref[...]
    mod = mod_ref[...]
    sh1, sc1, gt1 = mod[0:1], mod[1:2], mod[2:3]
    bd = bd_ref[...]
    rc, rs1, rs2 = rc_ref[...], rs1_ref[...], rs2_ref[...]

    ms = jnp.mean(x * x, axis=-1, keepdims=True)
    h1 = ((x * lax.rsqrt(ms + EPS)) * g1_ref[...]) * (1.0 + sc1) + sh1
    h1b = h1.astype(BF16)

    @pl.when(si == 0)
    def _():
        kk_ref[:, :, 0:WINDOW, :] = jnp.zeros((N_KV_HEADS, 2, WINDOW, LANES), BF16)
        vv_ref[:, :, 0:WINDOW, :] = jnp.zeros((N_KV_HEADS, 2, WINDOW, LANES), BF16)
        u_ref[0:8, :] = jnp.zeros((8, CONV_WIDTH), F32)

    @pl.when(si > 0)
    def _():
        kk_ref[:, :, 0:WINDOW, :] = kk_ref[:, :, ts:ts + WINDOW, :]
        vv_ref[:, :, 0:WINDOW, :] = vv_ref[:, :, ts:ts + WINDOW, :]
        u_ref[0:8, :] = u_ref[ts:ts + 8, :]

    def proj(lo, width):
        return jnp.dot(h1b, win_ref[:, lo:lo + width], preferred_element_type=F32)

    q = proj(0, ATTN_WIDTH)
    q = q * lax.rsqrt(_group_mean_sq(q, bd) + EPS) * gq_ref[...]
    qn_ref[...] = (_rope(q, rc, rs1, rs2) * ATTN_SCALE).astype(BF16)

    k = proj(ATTN_WIDTH, KV_WIDTH)
    k = k * lax.rsqrt(_group_mean_sq(k, bd) + EPS) * gk_ref[...]
    k = _rope(k, rc, rs1, rs2)
    v = proj(ATTN_WIDTH + KV_WIDTH, KV_WIDTH)
    low = lax.broadcasted_iota(jnp.int32, (1, LANES), 1) < HEAD_DIM
    for g in range(N_KV_HEADS):
        col = slice(LANES * (g // 2), LANES * (g // 2 + 1))
        for t, dst in ((k[:, col], kk_ref), (v[:, col], vv_ref)):
            tr = pltpu.roll(t, HEAD_DIM, 1)
            in_low, in_high = (t, tr) if g % 2 == 0 else (tr, t)
            dst[g, 0, WINDOW:WINDOW + ts, :] = jnp.where(low, in_low, 0.0).astype(BF16)
            dst[g, 1, WINDOW:WINDOW + ts, :] = jnp.where(low, 0.0, in_high).astype(BF16)

    qi = lax.broadcasted_iota(jnp.int32, (WINDOW, 2 * WINDOW), 0)
    kj = lax.broadcasted_iota(jnp.int32, (WINDOW, 2 * WINDOW), 1)
    band = (kj > qi) & (kj <= qi + WINDOW)
    for jb in range(ts // WINDOW):
        first_key = jnp.where(si * (ts // WINDOW) + jb == 0, WINDOW, 0)
        mask = band & (kj >= first_key)
        qrows = slice(jb * WINDOW, (jb + 1) * WINDOW)
        krows = slice(jb * WINDOW, (jb + 2) * WINDOW)
        for g in range(N_KV_HEADS):
            qa = qn_ref[qrows, MXU_DIM * g:MXU_DIM * g + LANES]
            qb = qn_ref[qrows, MXU_DIM * g + LANES:MXU_DIM * (g + 1)]
            q2 = jnp.concatenate([qa, qb], axis=0)
            s_lo = _dot_t(q2, kk_ref[g, 0, krows, :])
            s_hi = _dot_t(q2, kk_ref[g, 1, krows, :])
            vcat = jnp.concatenate([vv_ref[g, 0, krows, :], vv_ref[g, 1, krows, :]], axis=0)
            halves = []
            for half in range(2):
                rows = slice(half * WINDOW, (half + 1) * WINDOW)
                ps, inv = [], []
                for o, s in enumerate((s_lo[rows], s_hi[rows])):
                    sink = sinks_ref[Q_PER_KV * g + 2 * half + o]
                    s = jnp.where(mask, s, NEG_INF)
                    m = jnp.maximum(jnp.max(s, axis=-1, keepdims=True), sink)
                    p = jnp.exp(s - m)
                    denom = jnp.sum(p, axis=-1, keepdims=True) + jnp.exp(sink - m)
                    ps.append(p.astype(BF16))
                    inv.append(1.0 / denom)
                pv = jnp.dot(jnp.concatenate(ps, axis=1), vcat, preferred_element_type=F32)
                halves.append(pv * jnp.where(low, inv[0], inv[1]))
            att = jnp.concatenate(halves, axis=1)
            cols = slice(MXU_DIM * g, MXU_DIM * (g + 1))
            att = att * lax.rsqrt(_group_mean_sq(att, bd) + EPS) * goa_ref[:, cols]
            mixed_ref[qrows, cols] = att.astype(BF16)

    o0 = ATTN_WIDTH + 2 * KV_WIDTH
    bg = proj(o0, CONV_WIDTH)
    u_ref[8:8 + ts, :] = proj(o0 + CONV_WIDTH, CONV_WIDTH) * proj(o0 + 2 * CONV_WIDTH, CONV_WIDTH)
    cw = cw_ref[...]
    acc = cw[0:1] * u_ref[6:6 + ts, :]
    acc = acc + cw[1:2] * u_ref[7:7 + ts, :]
    acc = acc + cw[2:3] * u_ref[8:8 + ts, :]
    y = bg * acc
    y = y * lax.rsqrt(_group_mean_sq(y, bd) + EPS) * goc_ref[...]
    mixed_ref[:, ATTN_WIDTH:] = y.astype(BF16)

    o_ref[...] = x + gt1 * jnp.dot(mixed_ref[...], wout_ref[...], preferred_element_type=F32)


def _mixer(x, mod, g_norm1, w_in_b, gq, gk, rc, rs1, rs2, conv_w, goa, goc, w_out_b, bd, sinks):
    b, s, d = x.shape
    ts = min(MIX_ROWS, s)
    const = lambda shape: pl.BlockSpec(shape, lambda bi, si: (0,) * len(shape),
                                       pipeline_mode=pl.Buffered(1))
    rope_spec = pl.BlockSpec((ts, LANES), lambda bi, si: (si, 0))
    return pl.pallas_call(
        _mixer_kernel,
        grid=(b, s // ts),
        in_specs=[pl.BlockSpec(memory_space=pltpu.SMEM),
                  pl.BlockSpec((None, ts, d), lambda bi, si: (bi, si, 0)),
                  pl.BlockSpec((None, 6, d), lambda bi, si: (bi, 0, 0)),
                  const((1, d)),
                  const((d, IN_COLS)),
                  const((1, ATTN_WIDTH)),
                  const((1, KV_WIDTH)),
                  rope_spec, rope_spec, rope_spec,
                  const((CONV_K, CONV_WIDTH)),
                  const((1, ATTN_WIDTH)),
                  const((1, CONV_WIDTH)),
                  const((d, d)),
                  const((MXU_DIM, MXU_DIM))],
        out_specs=pl.BlockSpec((None, ts, d), lambda bi, si: (bi, si, 0)),
        out_shape=jax.ShapeDtypeStruct((b, s, d), F32),
        scratch_shapes=[pltpu.VMEM((N_KV_HEADS, 2, ts + WINDOW, LANES), BF16),
                        pltpu.VMEM((N_KV_HEADS, 2, ts + WINDOW, LANES), BF16),
                        pltpu.VMEM((ts + 8, CONV_WIDTH), F32),
                        pltpu.VMEM((ts, ATTN_WIDTH), BF16),
                        pltpu.VMEM((ts, d), BF16)],
        compiler_params=pltpu.CompilerParams(
            dimension_semantics=("arbitrary", "arbitrary"), vmem_limit_bytes=VMEM_LIMIT),
        name="mixer",
    )(sinks, x, mod, g_norm1, w_in_b, gq, gk, rc, rs1, rs2, conv_w, goa, goc, w_out_b, bd)


def _fold_kernel(sk_ref, w_ref, o_ref):
    o_ref[...] = _dot_t(sk_ref[...], w_ref[...]).astype(BF16)


def _fold(subkeys_b, w_pq_b):
    d = w_pq_b.shape[0]
    return pl.pallas_call(
        _fold_kernel,
        grid=(2, PEER_HEADS),
        in_specs=[pl.BlockSpec((None, None, PEER_NKEYS, PEER_HALF), lambda p, h: (h, p, 0, 0)),
                  pl.BlockSpec((d, PEER_HALF), lambda p, h: (0, 2 * h + p))],
        out_specs=pl.BlockSpec((PEER_NKEYS, d), lambda p, h: (p * PEER_HEADS + h, 0)),
        out_shape=jax.ShapeDtypeStruct((2 * PEER_HEADS * PEER_NKEYS, d), BF16),
        compiler_params=pltpu.CompilerParams(
            dimension_semantics=("arbitrary", "arbitrary"), vmem_limit_bytes=VMEM_LIMIT),
        name="fold",
    )(subkeys_b, w_pq_b)


def _top16(s):
    vals = []
    for it in range(PEER_TOPK):
        m = jnp.max(s, axis=0, keepdims=True)
        vals.append(m)
        if it + 1 < PEER_TOPK:
            s = jnp.where(s == m, -jnp.inf, s)
    return vals


def _route_kernel(x_ref, mod_ref, g2_ref, a_ref, h2_ref, s2_ref, e2w_ref, thr_ref, e1w_ref, st_ref):
    x = x_ref[...]
    mod = mod_ref[...]
    sh2, sc2 = mod[3:4], mod[4:5]
    ms = jnp.mean(x * x, axis=-1, keepdims=True)
    h2 = ((x * lax.rsqrt(ms + EPS)) * g2_ref[...]) * (1.0 + sc2) + sh2
    h2b = h2.astype(BF16)
    h2_ref[...] = h2b
    st_ref[...] = _dot_t(a_ref[...], h2b)

    hk = PEER_HEADS * PEER_NKEYS
    for g in range(x.shape[0] // LANES):
        lanes = slice(g * LANES, (g + 1) * LANES)
        t1, t2 = [], []
        for h in range(PEER_HEADS):
            t1.append(_top16(st_ref[h * PEER_NKEYS:(h + 1) * PEER_NKEYS, lanes]))
            t2.append(_top16(st_ref[hk + h * PEER_NKEYS:hk + (h + 1) * PEER_NKEYS, lanes]))
        r1 = [jnp.concatenate([t1[h][a] for h in range(PEER_HEADS)], axis=0) for a in range(PEER_TOPK)]
        r2 = [jnp.concatenate([t2[h][a] for h in range(PEER_HEADS)], axis=0) for a in range(PEER_TOPK)]
        zs = [r1[a] + r2[b] for (a, b) in _PAIRS]
        cur = list(zs)
        z16 = None
        for it in range(PEER_TOPK):
            z16 = functools.reduce(jnp.maximum, cur)
            if it + 1 < PEER_TOPK:
                cur = [jnp.where(c == z16, -jnp.inf, c) for c in cur]
        zmax = zs[0]
        zsum = jnp.zeros_like(zmax)
        cut = [jnp.full_like(zmax, jnp.inf) for _ in range(PEER_TOPK)]
        for (a, b), z in zip(_PAIRS, zs):
            sel = z >= z16
            zsum = zsum + jnp.where(sel, jnp.exp(z - zmax), 0.0)
            cut[a] = jnp.minimum(cut[a], jnp.where(sel, r2[b], jnp.inf))
        inv_z = 1.0 / zsum
        for h in range(PEER_HEADS):
            s1 = st_ref[h * PEER_NKEYS:(h + 1) * PEER_NKEYS, lanes]
            s2 = st_ref[hk + h * PEER_NKEYS:hk + (h + 1) * PEER_NKEYS, lanes]
            thr = jnp.full_like(s1, jnp.inf)
            for a in range(PEER_TOPK):
                thr = jnp.where(s1 == t1[h][a], cut[a][h:h + 1, :], thr)
            s2_ref[g, h] = s2
            e2w_ref[g, h] = jnp.exp(s2 - t2[h][0]) * inv_z[h:h + 1, :]
            thr_ref[g, h] = thr
            e1w_ref[g, h] = jnp.exp(s1 - t1[h][0])


def _route(x1, mod, g_norm2, a_mat):
    b, s, d = x1.shape
    n = b * s
    tr = min(ROUTE_ROWS, s)
    per_seq = s // tr
    grp = tr // LANES
    rshape = (n // LANES, PEER_HEADS, PEER_NKEYS, LANES)
    rspec = pl.BlockSpec((grp, PEER_HEADS, PEER_NKEYS, LANES), lambda i: (i, 0, 0, 0))
    return pl.pallas_call(
        _route_kernel,
        grid=(n // tr,),
        in_specs=[pl.BlockSpec((None, tr, d), lambda i: (i // per_seq, i % per_seq, 0)),
                  pl.BlockSpec((None, 6, d), lambda i: (i // per_seq, 0, 0)),
                  pl.BlockSpec((1, d), lambda i: (0, 0), pipeline_mode=pl.Buffered(1)),
                  pl.BlockSpec(a_mat.shape, lambda i: (0, 0), pipeline_mode=pl.Buffered(1))],
        out_specs=[pl.BlockSpec((tr, d), lambda i: (i, 0)), rspec, rspec, rspec, rspec],
        out_shape=[jax.ShapeDtypeStruct((n, d), BF16)] + [jax.ShapeDtypeStruct(rshape, F32)] * 4,
        scratch_shapes=[pltpu.VMEM((a_mat.shape[0], tr), F32)],
        compiler_params=pltpu.CompilerParams(
            dimension_semantics=("arbitrary",), vmem_limit_bytes=VMEM_LIMIT),
        name="route",
    )(x1, mod, g_norm2, a_mat)


def _peer_kernel(h2_ref, u_ref, vt_ref, s2_ref, e2w_ref, thr_ref, e1w_ref, yt_ref, a_scr, act_scr):
    c = pl.program_id(1)
    groups = a_scr.shape[0]
    at = _dot_t(u_ref[...], h2_ref[...])
    for tb in range(groups):
        a_scr[tb] = at[:, tb * LANES:(tb + 1) * LANES]

    sqrt_half = math.sqrt(0.5)

    def group_body(tb, carry):
        for e1l in range(PEER_E1):
            for eb in range(PEER_NKEYS // E2_ROWS):
                e2rows = slice(eb * E2_ROWS, (eb + 1) * E2_ROWS)
                w = jnp.zeros((E2_ROWS, LANES), F32)
                for h in range(PEER_HEADS):
                    cutoff = thr_ref[tb, h, e1l:e1l + 1, :]
                    w1 = e1w_ref[tb, h, e1l:e1l + 1, :]
                    w = w + w1 * jnp.where(s2_ref[tb, h, e2rows, :] >= cutoff,
                                           e2w_ref[tb, h, e2rows, :], 0.0)
                rows = slice(e1l * PEER_NKEYS + eb * E2_ROWS, e1l * PEER_NKEYS + (eb + 1) * E2_ROWS)
                a = a_scr[tb, rows, :]
                gelu = 0.5 * a * (1.0 + lax.erf(a * sqrt_half))
                act_scr[tb, rows, :] = (gelu * w).astype(BF16)
        return carry

    lax.fori_loop(0, groups, group_body, 0)
    act = jnp.concatenate([act_scr[tb] for tb in range(groups)], axis=1)
    contrib = jnp.dot(vt_ref[...], act, preferred_element_type=F32)

    @pl.when(c == 0)
    def _():
        yt_ref[...] = contrib

    @pl.when(c > 0)
    def _():
        yt_ref[...] += contrib


def _peer(h2, u_b, vt_b, s2, e2w, thr, e1w):
    n, d = h2.shape
    tp = min(PEER_ROWS, n)
    grp = tp // LANES
    ec = PEER_E1 * PEER_NKEYS
    full = pl.BlockSpec((grp, PEER_HEADS, PEER_NKEYS, LANES), lambda i, c: (i, 0, 0, 0))
    part = pl.BlockSpec((grp, PEER_HEADS, PEER_E1, LANES), lambda i, c: (i, 0, c, 0))
    return pl.pallas_call(
        _peer_kernel,
        grid=(n // tp, PEER_N // ec),
        in_specs=[pl.BlockSpec((tp, d), lambda i, c: (i, 0)),
                  pl.BlockSpec((ec, d), lambda i, c: (c, 0)),
                  pl.BlockSpec((d, ec), lambda i, c: (0, c)),
                  full, full, part, part],
        out_specs=pl.BlockSpec((d, tp), lambda i, c: (0, i)),
        out_shape=jax.ShapeDtypeStruct((d, n), F32),
        scratch_shapes=[pltpu.VMEM((grp, ec, LANES), F32),
                        pltpu.VMEM((grp, ec, LANES), BF16)],
        compiler_params=pltpu.CompilerParams(
            dimension_semantics=("arbitrary", "arbitrary"), vmem_limit_bytes=VMEM_LIMIT),
        name="peer",
    )(h2, u_b, vt_b, s2, e2w, thr, e1w)


def _final_kernel(x_ref, mod_ref, yt_ref, o_ref):
    gt2 = mod_ref[...][5:6]
    o_ref[...] = x_ref[...] + gt2 * yt_ref[...].T


def _final(x1, mod, yt):
    b, s, d = x1.shape
    tf = min(PEER_ROWS, s)
    per_seq = s // tf
    return pl.pallas_call(
        _final_kernel,
        grid=(b, per_seq),
        in_specs=[pl.BlockSpec((None, tf, d), lambda bi, si: (bi, si, 0)),
                  pl.BlockSpec((None, 6, d), lambda bi, si: (bi, 0, 0)),
                  pl.BlockSpec((d, tf), lambda bi, si: (0, bi * per_seq + si))],
        out_specs=pl.BlockSpec((None, tf, d), lambda bi, si: (bi, si, 0)),
        out_shape=jax.ShapeDtypeStruct((b, s, d), F32),
        compiler_params=pltpu.CompilerParams(
            dimension_semantics=("arbitrary", "arbitrary"), vmem_limit_bytes=VMEM_LIMIT),
        name="final",
    )(x1, mod, yt)


def _rope_tables(s):
    pos = jnp.arange(s, dtype=F32)
    inv_freq = ROPE_THETA ** (-jnp.arange(0, ROPE_DIM, 2, dtype=F32) / ROPE_DIM)
    ang = pos[:, None] * inv_freq[None, :]
    cos, sin = jnp.cos(ang), jnp.sin(ang)
    half = ROPE_DIM // 2
    ones = jnp.ones((s, HEAD_DIM - ROPE_DIM), F32)
    zeros = jnp.zeros((s, HEAD_DIM - ROPE_DIM), F32)
    zh = jnp.zeros((s, half), F32)
    rc = jnp.concatenate([cos, cos, ones], axis=1)
    rs1 = jnp.concatenate([-sin, zh, zeros], axis=1)
    rs2 = jnp.concatenate([zh, sin, zeros], axis=1)
    tile = lambda t: jnp.tile(t, (1, LANES // HEAD_DIM))
    return tile(rc), tile(rs1), tile(rs2)


def kernel(x, c, w_ada, b_ada, g_norm1, w_in, g_q, g_k, sinks, conv_w, g_out_attn,
           g_out_conv, w_out, g_norm2, w_pq, peer_subkeys, peer_u, peer_v):
    b, s, d = x.shape
    assert d == D_MODEL and s % WINDOW == 0 and (b * s) % LANES == 0

    mod = _ada(c, w_ada, b_ada).reshape(b, 6, d)

    rc, rs1, rs2 = _rope_tables(s)
    blk = jnp.arange(MXU_DIM) // HEAD_DIM
    bd = (blk[:, None] == blk[None, :]).astype(BF16)
    x1 = _mixer(x, mod, g_norm1.reshape(1, d), w_in.astype(BF16),
                jnp.tile(g_q, N_Q_HEADS).reshape(1, ATTN_WIDTH),
                jnp.tile(g_k, N_KV_HEADS).reshape(1, KV_WIDTH),
                rc, rs1, rs2, conv_w, g_out_attn.reshape(1, ATTN_WIDTH),
                g_out_conv.reshape(1, CONV_WIDTH), w_out.astype(BF16), bd, sinks)

    a_mat = _fold(peer_subkeys.astype(BF16), w_pq.astype(BF16))
    h2, s2, e2w, thr, e1w = _route(x1, mod, g_norm2.reshape(1, d), a_mat)
    yt = _peer(h2, peer_u.astype(BF16), peer_v.T.astype(BF16), s2, e2w, thr, e1w)
    return _final(x1, mod, yt)
```

```python
import functools
import math

import jax
import jax.numpy as jnp
from jax import lax
from jax.experimental import pallas as pl
from jax.experimental.pallas import tpu as pltpu

F32 = jnp.float32
BF16 = jnp.bfloat16

D_MODEL = 2048
HEAD_DIM = 64
N_Q_HEADS = 16
N_KV_HEADS = 4
Q_PER_KV = N_Q_HEADS // N_KV_HEADS
ATTN_WIDTH = N_Q_HEADS * HEAD_DIM
KV_WIDTH = N_KV_HEADS * HEAD_DIM
CONV_WIDTH = D_MODEL - ATTN_WIDTH
CONV_K = 3
IN_COLS = ATTN_WIDTH + 2 * KV_WIDTH + 3 * CONV_WIDTH
WINDOW = 128
ROPE_THETA = 500000.0
ROPE_DIM = HEAD_DIM // 4
ATTN_SCALE = HEAD_DIM ** -0.5
NEG_INF = -1e30
PEER_HEADS = 8
PEER_NKEYS = 128
PEER_N = PEER_NKEYS * PEER_NKEYS
PEER_HALF = 128
PEER_TOPK = 16
EPS = 1e-6

LANES = 128
MXU_DIM = 256
VMEM_LIMIT = 56 * 1024 * 1024

ADA_COLS = 1024
MIX_ROWS = 256
ROUTE_ROWS = 256
PEER_ROWS = 512
PEER_E1 = 8
E2_ROWS = 32

_PAIRS = tuple((a, b) for a in range(PEER_TOPK) for b in range(PEER_TOPK)
               if (a + 1) * (b + 1) <= PEER_TOPK)


def _dot_t(a, b):
    return lax.dot_general(a, b, (((1,), (1,)), ((), ())),
                           preferred_element_type=F32)


def _group_mean_sq(y, bd):
    y2 = y * y
    hi = y2.astype(BF16)
    lo = (y2 - hi.astype(F32)).astype(BF16)
    outs = []
    for j in range(y.shape[1] // MXU_DIM):
        sl = slice(MXU_DIM * j, MXU_DIM * (j + 1))
        outs.append(jnp.dot(hi[:, sl], bd, preferred_element_type=F32)
                    + jnp.dot(lo[:, sl], bd, preferred_element_type=F32))
    ss = outs[0] if len(outs) == 1 else jnp.concatenate(outs, axis=1)
    return ss * (1.0 / HEAD_DIM)


def _rope(t, rc, rs1, rs2):
    outs = []
    for j in range(t.shape[1] // LANES):
        tj = t[:, LANES * j:LANES * (j + 1)]
        outs.append(tj * rc
                    + pltpu.roll(tj, LANES - ROPE_DIM // 2, 1) * rs1
                    + pltpu.roll(tj, ROPE_DIM // 2, 1) * rs2)
    return outs[0] if len(outs) == 1 else jnp.concatenate(outs, axis=1)


def _ada_kernel(c_ref, w_ref, b_ref, o_ref):
    c = c_ref[...]
    s = c * jax.nn.sigmoid(c)
    o_ref[...] = jnp.dot(s, w_ref[...], preferred_element_type=F32,
                         precision=lax.Precision.HIGHEST) + b_ref[...]


def _ada(c, w_ada, b_ada):
    b, d = c.shape
    n = w_ada.shape[1]
    return pl.pallas_call(
        _ada_kernel,
        grid=(n // ADA_COLS,),
        in_specs=[pl.BlockSpec((b, d), lambda j: (0, 0)),
                  pl.BlockSpec((d, ADA_COLS), lambda j: (0, j)),
                  pl.BlockSpec((1, ADA_COLS), lambda j: (0, j))],
        out_specs=pl.BlockSpec((b, ADA_COLS), lambda j: (0, j)),
        out_shape=jax.ShapeDtypeStruct((b, n), F32),
        compiler_params=pltpu.CompilerParams(
            dimension_semantics=("arbitrary",), vmem_limit_bytes=VMEM_LIMIT),
        name="ada",
    )(c, w_ada, b_ada.reshape(1, n))


def _mixer_kernel(sinks_ref, x_ref, mod_ref, g1_ref, win_ref, gq_ref, gk_ref,
                  rc_ref, rs1_ref, rs2_ref, cw_ref, goa_ref, goc_ref, wout_ref,
                  bd_ref, o_ref, kk_ref, vv_ref, u_ref, qn_ref, mixed_ref):
    ts = x_ref.shape[0]
    si = pl.program_id(1)
    x = x_ref[...]
    mod = mod_ref[...]
    sh1, sc1, gt1 = mod[0:1], mod[1:2], mod[2:3]
    bd = bd_ref[...]
    rc, rs1, rs2 = rc_ref[...], rs1_ref[...], rs2_ref[...]

    ms = jnp.mean(x * x, axis=-1, keepdims=True)
    h1 = ((x * lax.rsqrt(ms + EPS)) * g1_ref[...]) * (1.0 + sc1) + sh1
    h1b = h1.astype(BF16)

    @pl.when(si == 0)
    def _():
        kk_ref[:, :, 0:WINDOW, :] = jnp.zeros((N_KV_HEADS, 2, WINDOW, LANES), BF16)
        vv_ref[:, :, 0:WINDOW, :] = jnp.zeros((N_KV_HEADS, 2, WINDOW, LANES), BF16)
        u_ref[0:8, :] = jnp.zeros((8, CONV_WIDTH), F32)

    @pl.when(si > 0)
    def _():
        kk_ref[:, :, 0:WINDOW, :] = kk_ref[:, :, ts:ts + WINDOW, :]
        vv_ref[:, :, 0:WINDOW, :] = vv_ref[:, :, ts:ts + WINDOW, :]
        u_ref[0:8, :] = u_ref[ts:ts + 8, :]

    def proj(lo, width):
        return jnp.dot(h1b, win_ref[:, lo:lo + width], preferred_element_type=F32)

    q = proj(0, ATTN_WIDTH)
    q = q * lax.rsqrt(_group_mean_sq(q, bd) + EPS) * gq_ref[...]
    qn_ref[...] = (_rope(q, rc, rs1, rs2) * ATTN_SCALE).astype(BF16)

    k = proj(ATTN_WIDTH, KV_WIDTH)
    k = k * lax.rsqrt(_group_mean_sq(k, bd) + EPS) * gk_ref[...]
    k = _rope(k, rc, rs1, rs2)
    v = proj(ATTN_WIDTH + KV_WIDTH, KV_WIDTH)
    low = lax.broadcasted_iota(jnp.int32, (1, LANES), 1) < HEAD_DIM
    for g in range(N_KV_HEADS):
        col = slice(LANES * (g // 2), LANES * (g // 2 + 1))
        for t, dst in ((k[:, col], kk_ref), (v[:, col], vv_ref)):
            tr = pltpu.roll(t, HEAD_DIM, 1)
            in_low, in_high = (t, tr) if g % 2 == 0 else (tr, t)
            dst[g, 0, WINDOW:WINDOW + ts, :] = jnp.where(low, in_low, 0.0).astype(BF16)
            dst[g, 1, WINDOW:WINDOW + ts, :] = jnp.where(low, 0.0, in_high).astype(BF16)

    qi = lax.broadcasted_iota(jnp.int32, (WINDOW, 2 * WINDOW), 0)
    kj = lax.broadcasted_iota(jnp.int32, (WINDOW, 2 * WINDOW), 1)
    band = (kj > qi) & (kj <= qi + WINDOW)
    for jb in range(ts // WINDOW):
        first_key = jnp.where(si * (ts // WINDOW) + jb == 0, WINDOW, 0)
        mask = band & (kj >= first_key)
        qrows = slice(jb * WINDOW, (jb + 1) * WINDOW)
        krows = slice(jb * WINDOW, (jb + 2) * WINDOW)
        for g in range(N_KV_HEADS):
            qa = qn_ref[qrows, MXU_DIM * g:MXU_DIM * g + LANES]
            qb = qn_ref[qrows, MXU_DIM * g + LANES:MXU_DIM * (g + 1)]
            q2 = jnp.concatenate([qa, qb], axis=0)
            s_lo = _dot_t(q2, kk_ref[g, 0, krows, :])
            s_hi = _dot_t(q2, kk_ref[g, 1, krows, :])
            vcat = jnp.concatenate([vv_ref[g, 0, krows, :], vv_ref[g, 1, krows, :]], axis=0)
            halves = []
            for half in range(2):
                rows = slice(half * WINDOW, (half + 1) * WINDOW)
                ps, inv = [], []
                for o, s in enumerate((s_lo[rows], s_hi[rows])):
                    sink = sinks_ref[Q_PER_KV * g + 2 * half + o]
                    s = jnp.where(mask, s, NEG_INF)
                    m = jnp.maximum(jnp.max(s, axis=-1, keepdims=True), sink)
                    p = jnp.exp(s - m)
                    denom = jnp.sum(p, axis=-1, keepdims=True) + jnp.exp(sink - m)
                    ps.append(p.astype(BF16))
                    inv.append(1.0 / denom)
                pv = jnp.dot(jnp.concatenate(ps, axis=1), vcat, preferred_element_type=F32)
                halves.append(pv * jnp.where(low, inv[0], inv[1]))
            att = jnp.concatenate(halves, axis=1)
            cols = slice(MXU_DIM * g, MXU_DIM * (g + 1))
            att = att * lax.rsqrt(_group_mean_sq(att, bd) + EPS) * goa_ref[:, cols]
            mixed_ref[qrows, cols] = att.astype(BF16)

    o0 = ATTN_WIDTH + 2 * KV_WIDTH
    bg = proj(o0, CONV_WIDTH)
    u_ref[8:8 + ts, :] = proj(o0 + CONV_WIDTH, CONV_WIDTH) * proj(o0 + 2 * CONV_WIDTH, CONV_WIDTH)
    cw = cw_ref[...]
    acc = cw[0:1] * u_ref[6:6 + ts, :]
    acc = acc + cw[1:2] * u_ref[7:7 + ts, :]
    acc = acc + cw[2:3] * u_ref[8:8 + ts, :]
    y = bg * acc
    y = y * lax.rsqrt(_group_mean_sq(y, bd) + EPS) * goc_ref[...]
    mixed_ref[:, ATTN_WIDTH:] = y.astype(BF16)

    o_ref[...] = x + gt1 * jnp.dot(mixed_ref[...], wout_ref[...], preferred_element_type=F32)


def _mixer(x, mod, g_norm1, w_in_b, gq, gk, rc, rs1, rs2, conv_w, goa, goc, w_out_b, bd, sinks):
    b, s, d = x.shape
    ts = min(MIX_ROWS, s)
    const = lambda shape: pl.BlockSpec(shape, lambda bi, si: (0,) * len(shape),
                                       pipeline_mode=pl.Buffered(1))
    rope_spec = pl.BlockSpec((ts, LANES), lambda bi, si: (si, 0))
    return pl.pallas_call(
        _mixer_kernel,
        grid=(b, s // ts),
        in_specs=[pl.BlockSpec(memory_space=pltpu.SMEM),
                  pl.BlockSpec((None, ts, d), lambda bi, si: (bi, si, 0)),
                  pl.BlockSpec((None, 6, d), lambda bi, si: (bi, 0, 0)),
                  const((1, d)),
                  const((d, IN_COLS)),
                  const((1, ATTN_WIDTH)),
                  const((1, KV_WIDTH)),
                  rope_spec, rope_spec, rope_spec,
                  const((CONV_K, CONV_WIDTH)),
                  const((1, ATTN_WIDTH)),
                  const((1, CONV_WIDTH)),
                  const((d, d)),
                  const((MXU_DIM, MXU_DIM))],
        out_specs=pl.BlockSpec((None, ts, d), lambda bi, si: (bi, si, 0)),
        out_shape=jax.ShapeDtypeStruct((b, s, d), F32),
        scratch_shapes=[pltpu.VMEM((N_KV_HEADS, 2, ts + WINDOW, LANES), BF16),
                        pltpu.VMEM((N_KV_HEADS, 2, ts + WINDOW, LANES), BF16),
                        pltpu.VMEM((ts + 8, CONV_WIDTH), F32),
                        pltpu.VMEM((ts, ATTN_WIDTH), BF16),
                        pltpu.VMEM((ts, d), BF16)],
        compiler_params=pltpu.CompilerParams(
            dimension_semantics=("arbitrary", "arbitrary"), vmem_limit_bytes=VMEM_LIMIT),
        name="mixer",
    )(sinks, x, mod, g_norm1, w_in_b, gq, gk, rc, rs1, rs2, conv_w, goa, goc, w_out_b, bd)


def _fold_kernel(sk_ref, w_ref, o_ref):
    o_ref[...] = _dot_t(sk_ref[...], w_ref[...]).astype(BF16)


def _fold(subkeys_b, w_pq_b):
    d = w_pq_b.shape[0]
    return pl.pallas_call(
        _fold_kernel,
        grid=(2, PEER_HEADS),
        in_specs=[pl.BlockSpec((None, None, PEER_NKEYS, PEER_HALF), lambda p, h: (h, p, 0, 0)),
                  pl.BlockSpec((d, PEER_HALF), lambda p, h: (0, 2 * h + p))],
        out_specs=pl.BlockSpec((PEER_NKEYS, d), lambda p, h: (p * PEER_HEADS + h, 0)),
        out_shape=jax.ShapeDtypeStruct((2 * PEER_HEADS * PEER_NKEYS, d), BF16),
        compiler_params=pltpu.CompilerParams(
            dimension_semantics=("arbitrary", "arbitrary"), vmem_limit_bytes=VMEM_LIMIT),
        name="fold",
    )(subkeys_b, w_pq_b)


def _top16(s, want_rank=False):
    vals = []
    rank = jnp.full(s.shape, float(PEER_TOPK), F32) if want_rank else None
    for it in range(PEER_TOPK):
        m = jnp.max(s, axis=0, keepdims=True)
        vals.append(m)
        hit = s == m
        if want_rank:
            rank = jnp.where(hit, float(it), rank)
        if it + 1 < PEER_TOPK:
            s = jnp.where(hit, -jnp.inf, s)
    return (vals, rank) if want_rank else vals


def _route_kernel(x_ref, mod_ref, g2_ref, a_ref, h2_ref, rank2_ref, e2w_ref, cnt_ref, e1w_ref, st_ref):
    x = x_ref[...]
    mod = mod_ref[...]
    sh2, sc2 = mod[3:4], mod[4:5]
    ms = jnp.mean(x * x, axis=-1, keepdims=True)
    h2 = ((x * lax.rsqrt(ms + EPS)) * g2_ref[...]) * (1.0 + sc2) + sh2
    h2b = h2.astype(BF16)
    h2_ref[...] = h2b
    st_ref[...] = _dot_t(a_ref[...], h2b)

    hk = PEER_HEADS * PEER_NKEYS
    for g in range(x.shape[0] // LANES):
        lanes = slice(g * LANES, (g + 1) * LANES)
        t1, t2 = [], []
        for h in range(PEER_HEADS):
            t1.append(_top16(st_ref[h * PEER_NKEYS:(h + 1) * PEER_NKEYS, lanes]))
            vals, rank = _top16(st_ref[hk + h * PEER_NKEYS:hk + (h + 1) * PEER_NKEYS, lanes], True)
            t2.append(vals)
            rank2_ref[g, h] = rank.astype(BF16)
        r1 = [jnp.concatenate([t1[h][a] for h in range(PEER_HEADS)], axis=0) for a in range(PEER_TOPK)]
        r2 = [jnp.concatenate([t2[h][a] for h in range(PEER_HEADS)], axis=0) for a in range(PEER_TOPK)]
        zs = [r1[a] + r2[b] for (a, b) in _PAIRS]
        cur = list(zs)
        z16 = None
        for it in range(PEER_TOPK):
            z16 = functools.reduce(jnp.maximum, cur)
            if it + 1 < PEER_TOPK:
                cur = [jnp.where(c == z16, -jnp.inf, c) for c in cur]
        zmax = zs[0]
        zsum = jnp.zeros_like(zmax)
        cnt = [jnp.zeros_like(zmax) for _ in range(PEER_TOPK)]
        for (a, b), z in zip(_PAIRS, zs):
            sel = z >= z16
            zsum = zsum + jnp.where(sel, jnp.exp(z - zmax), 0.0)
            cnt[a] = cnt[a] + jnp.where(sel, 1.0, 0.0)
        inv_z = 1.0 / zsum
        for h in range(PEER_HEADS):
            s1 = st_ref[h * PEER_NKEYS:(h + 1) * PEER_NKEYS, lanes]
            s2 = st_ref[hk + h * PEER_NKEYS:hk + (h + 1) * PEER_NKEYS, lanes]
            cnt_full = jnp.zeros_like(s1)
            for a in range(PEER_TOPK):
                cnt_full = jnp.where(s1 == t1[h][a], cnt[a][h:h + 1, :], cnt_full)
            e2w_ref[g, h] = (jnp.exp(s2 - t2[h][0]) * inv_z[h:h + 1, :]).astype(BF16)
            cnt_ref[g, h] = cnt_full
            e1w_ref[g, h] = jnp.exp(s1 - t1[h][0])


def _route(x1, mod, g_norm2, a_mat):
    b, s, d = x1.shape
    n = b * s
    tr = min(ROUTE_ROWS, s)
    per_seq = s // tr
    grp = tr // LANES
    rshape = (n // LANES, PEER_HEADS, PEER_NKEYS, LANES)
    rspec = pl.BlockSpec((grp, PEER_HEADS, PEER_NKEYS, LANES), lambda i: (i, 0, 0, 0))
    return pl.pallas_call(
        _route_kernel,
        grid=(n // tr,),
        in_specs=[pl.BlockSpec((None, tr, d), lambda i: (i // per_seq, i % per_seq, 0)),
                  pl.BlockSpec((None, 6, d), lambda i: (i // per_seq, 0, 0)),
                  pl.BlockSpec((1, d), lambda i: (0, 0), pipeline_mode=pl.Buffered(1)),
                  pl.BlockSpec(a_mat.shape, lambda i: (0, 0), pipeline_mode=pl.Buffered(1))],
        out_specs=[pl.BlockSpec((tr, d), lambda i: (i, 0)), rspec, rspec, rspec, rspec],
        out_shape=[jax.ShapeDtypeStruct((n, d), BF16),
                   jax.ShapeDtypeStruct(rshape, BF16), jax.ShapeDtypeStruct(rshape, BF16),
                   jax.ShapeDtypeStruct(rshape, F32), jax.ShapeDtypeStruct(rshape, F32)],
        scratch_shapes=[pltpu.VMEM((a_mat.shape[0], tr), F32)],
        compiler_params=pltpu.CompilerParams(
            dimension_semantics=("arbitrary",), vmem_limit_bytes=VMEM_LIMIT),
        name="route",
    )(x1, mod, g_norm2, a_mat)


def _peer_kernel(h2_ref, u_ref, vt_ref, rank2_ref, e2w_ref, cnt_ref, e1w_ref, yt_ref,
                 a0_scr, a1_scr, act0_scr, act1_scr):
    g = pl.program_id(0)
    groups, ec = a0_scr.shape[0], a0_scr.shape[1]
    d = yt_ref.shape[0]
    n_chunks = PEER_NKEYS // PEER_E1

    @pl.when(g == 0)
    def _():
        for ref in (a0_scr, a1_scr, act0_scr, act1_scr):
            ref[...] = jnp.zeros(ref.shape, ref.dtype)

    @pl.when((g == 0) | ((g >= 2) & ((g - 2) % n_chunks == 0)))
    def _():
        yt_ref[...] = jnp.zeros(yt_ref.shape, F32)

    sqrt_half = math.sqrt(0.5)

    def run(a_new, a_old, act_new, act_old):
        nblk = PEER_NKEYS // E2_ROWS

        def gate_thunks(e1l, tb):
            w = [jnp.zeros((E2_ROWS, LANES), BF16) for _ in range(nblk)]

            def head_part(h):
                cnt = jnp.broadcast_to(cnt_ref[tb, h, e1l:e1l + 1, :], (E2_ROWS, LANES)).astype(BF16)
                w1 = jnp.broadcast_to(e1w_ref[tb, h, e1l:e1l + 1, :], (E2_ROWS, LANES)).astype(BF16)
                for eb in range(nblk):
                    e2rows = slice(eb * E2_ROWS, (eb + 1) * E2_ROWS)
                    w[eb] = w[eb] + w1 * jnp.where(rank2_ref[tb, h, e2rows, :] < cnt,
                                                   e2w_ref[tb, h, e2rows, :], jnp.zeros((), BF16))

            def finish_part(eb):
                rows = slice(e1l * PEER_NKEYS + eb * E2_ROWS,
                             e1l * PEER_NKEYS + (eb + 1) * E2_ROWS)
                a = a_old[tb, rows, :]
                gelu = 0.5 * a * (1.0 + lax.erf(a * sqrt_half))
                act_new[tb, rows, :] = gelu.astype(BF16) * w[eb]

            return ([functools.partial(head_part, h) for h in range(PEER_HEADS)]
                    + [functools.partial(finish_part, eb) for eb in range(nblk)])

        thunks = [t for e1l in range(PEER_E1) for tb in range(groups) for t in gate_thunks(e1l, tb)]

        halves = groups // 2
        kt_score, kt_value = d // MXU_DIM, ec // MXU_DIM
        mxu_ops = []
        for rb in range(ec // MXU_DIM):
            for nb in range(halves):
                mxu_ops.append(("score", rb, nb))
        for rb in range(d // (2 * MXU_DIM)):
            for nb in range(halves):
                mxu_ops.append(("value", rb, nb))
        n_small = len(mxu_ops) // 2 * (kt_score + 2 * kt_value)
        done_small, done_thunks = 0, 0

        def vpu_fill():
            nonlocal done_thunks
            target = -(-len(thunks) * done_small // n_small)
            while done_thunks < min(target, len(thunks)):
                thunks[done_thunks]()
                done_thunks += 1

        for kind, rb, nb in mxu_ops:
            tok = slice(nb * MXU_DIM, (nb + 1) * MXU_DIM)
            if kind == "score":
                rows = slice(rb * MXU_DIM, (rb + 1) * MXU_DIM)
                acc = None
                for k in range(kt_score):
                    ks = slice(k * MXU_DIM, (k + 1) * MXU_DIM)
                    part = _dot_t(u_ref[rows, ks], h2_ref[tok, ks])
                    acc = part if acc is None else acc + part
                    done_small += 1
                    vpu_fill()
                a_new[2 * nb, rows, :] = acc[:, :LANES]
                a_new[2 * nb + 1, rows, :] = acc[:, LANES:]
            else:
                rows = slice(rb * 2 * MXU_DIM, (rb + 1) * 2 * MXU_DIM)
                acc = None
                for k in range(kt_value):
                    ks = slice(k * MXU_DIM, (k + 1) * MXU_DIM)
                    act = jnp.concatenate([act_old[2 * nb, ks, :], act_old[2 * nb + 1, ks, :]], axis=1)
                    part = jnp.dot(vt_ref[rows, ks], act, preferred_element_type=F32)
                    acc = part if acc is None else acc + part
                    done_small += 2
                    vpu_fill()
                yt_ref[rows, tok] += acc
        assert done_thunks == len(thunks)

    @pl.when(g % 2 == 0)
    def _():
        run(a0_scr, a1_scr, act1_scr, act0_scr)

    @pl.when(g % 2 == 1)
    def _():
        run(a1_scr, a0_scr, act0_scr, act1_scr)


def _peer(h2, u_b, vt_b, rank2, e2w, cnt, e1w):
    n, d = h2.shape
    tp = min(PEER_ROWS, n)
    grp = tp // LANES
    ec = PEER_E1 * PEER_NKEYS
    nc = PEER_N // ec
    total = (n // tp) * nc
    lag = lambda g, k: jnp.clip(g - k, 0, total - 1)
    full = pl.BlockSpec((grp, PEER_HEADS, PEER_NKEYS, LANES), lambda g: (lag(g, 1) // nc, 0, 0, 0))
    part = pl.BlockSpec((grp, PEER_HEADS, PEER_E1, LANES),
                        lambda g: (lag(g, 1) // nc, 0, lag(g, 1) % nc, 0))
    return pl.pallas_call(
        _peer_kernel,
        grid=(total + 2,),
        in_specs=[pl.BlockSpec((tp, d), lambda g: (lag(g, 0) // nc, 0)),
                  pl.BlockSpec((ec, d), lambda g: (lag(g, 0) % nc, 0)),
                  pl.BlockSpec((d, ec), lambda g: (0, lag(g, 2) % nc)),
                  full, full, part, part],
        out_specs=pl.BlockSpec((d, tp), lambda g: (0, lag(g, 2) // nc)),
        out_shape=jax.ShapeDtypeStruct((d, n), F32),
        scratch_shapes=[pltpu.VMEM((grp, ec, LANES), F32), pltpu.VMEM((grp, ec, LANES), F32),
                        pltpu.VMEM((grp, ec, LANES), BF16), pltpu.VMEM((grp, ec, LANES), BF16)],
        compiler_params=pltpu.CompilerParams(
            dimension_semantics=("arbitrary",), vmem_limit_bytes=VMEM_LIMIT),
        name="peer",
    )(h2, u_b, vt_b, rank2, e2w, cnt, e1w)


def _final_kernel(x_ref, mod_ref, yt_ref, o_ref):
    gt2 = mod_ref[...][5:6]
    o_ref[...] = x_ref[...] + gt2 * yt_ref[...].T


def _final(x1, mod, yt):
    b, s, d = x1.shape
    tf = min(PEER_ROWS, s)
    per_seq = s // tf
    return pl.pallas_call(
        _final_kernel,
        grid=(b, per_seq),
        in_specs=[pl.BlockSpec((None, tf, d), lambda bi, si: (bi, si, 0)),
                  pl.BlockSpec((None, 6, d), lambda bi, si: (bi, 0, 0)),
                  pl.BlockSpec((d, tf), lambda bi, si: (0, bi * per_seq + si))],
        out_specs=pl.BlockSpec((None, tf, d), lambda bi, si: (bi, si, 0)),
        out_shape=jax.ShapeDtypeStruct((b, s, d), F32),
        compiler_params=pltpu.CompilerParams(
            dimension_semantics=("arbitrary", "arbitrary"), vmem_limit_bytes=VMEM_LIMIT),
        name="final",
    )(x1, mod, yt)


def _rope_tables(s):
    pos = jnp.arange(s, dtype=F32)
    inv_freq = ROPE_THETA ** (-jnp.arange(0, ROPE_DIM, 2, dtype=F32) / ROPE_DIM)
    ang = pos[:, None] * inv_freq[None, :]
    cos, sin = jnp.cos(ang), jnp.sin(ang)
    half = ROPE_DIM // 2
    ones = jnp.ones((s, HEAD_DIM - ROPE_DIM), F32)
    zeros = jnp.zeros((s, HEAD_DIM - ROPE_DIM), F32)
    zh = jnp.zeros((s, half), F32)
    rc = jnp.concatenate([cos, cos, ones], axis=1)
    rs1 = jnp.concatenate([-sin, zh, zeros], axis=1)
    rs2 = jnp.concatenate([zh, sin, zeros], axis=1)
    tile = lambda t: jnp.tile(t, (1, LANES // HEAD_DIM))
    return tile(rc), tile(rs1), tile(rs2)


def kernel(x, c, w_ada, b_ada, g_norm1, w_in, g_q, g_k, sinks, conv_w, g_out_attn,
           g_out_conv, w_out, g_norm2, w_pq, peer_subkeys, peer_u, peer_v):
    b, s, d = x.shape
    assert d == D_MODEL and s % WINDOW == 0 and (b * s) % LANES == 0

    mod = _ada(c, w_ada, b_ada).reshape(b, 6, d)

    rc, rs1, rs2 = _rope_tables(s)
    blk = jnp.arange(MXU_DIM) // HEAD_DIM
    bd = (blk[:, None] == blk[None, :]).astype(BF16)
    x1 = _mixer(x, mod, g_norm1.reshape(1, d), w_in.astype(BF16),
                jnp.tile(g_q, N_Q_HEADS).reshape(1, ATTN_WIDTH),
                jnp.tile(g_k, N_KV_HEADS).reshape(1, KV_WIDTH),
                rc, rs1, rs2, conv_w, g_out_attn.reshape(1, ATTN_WIDTH),
                g_out_conv.reshape(1, CONV_WIDTH), w_out.astype(BF16), bd, sinks)

    a_mat = _fold(peer_subkeys.astype(BF16), w_pq.astype(BF16))
    h2, rank2, e2w, cnt, e1w = _route(x1, mod, g_norm2.reshape(1, d), a_mat)
    yt = _peer(h2, peer_u.astype(BF16), peer_v.T.astype(BF16), rank2, e2w, cnt, e1w)
    return _final(x1, mod, yt)
```

```python
import functools
import math

import jax
import jax.numpy as jnp
from jax import lax
from jax.experimental import pallas as pl
from jax.experimental.pallas import tpu as pltpu

F32 = jnp.float32
BF16 = jnp.bfloat16

D_MODEL = 2048
HEAD_DIM = 64
N_Q_HEADS = 16
N_KV_HEADS = 4
Q_PER_KV = N_Q_HEADS // N_KV_HEADS
ATTN_WIDTH = N_Q_HEADS * HEAD_DIM
KV_WIDTH = N_KV_HEADS * HEAD_DIM
CONV_WIDTH = D_MODEL - ATTN_WIDTH
CONV_K = 3
IN_COLS = ATTN_WIDTH + 2 * KV_WIDTH + 3 * CONV_WIDTH
WINDOW = 128
ROPE_THETA = 500000.0
ROPE_DIM = HEAD_DIM // 4
ATTN_SCALE = HEAD_DIM ** -0.5
NEG_INF = -1e30
PEER_HEADS = 8
PEER_NKEYS = 128
PEER_N = PEER_NKEYS * PEER_NKEYS
PEER_HALF = 128
PEER_TOPK = 16
EPS = 1e-6

LANES = 128
MXU_DIM = 256
VMEM_LIMIT = 56 * 1024 * 1024

ADA_COLS = 1024
MIX_ROWS = 256
ROUTE_ROWS = 256
PEER_ROWS = 512
PEER_E1 = 8
E2_ROWS = 32
SCORE_ROWS = 512
VALUE_ROWS = 512

_PAIRS = tuple((a, b) for a in range(PEER_TOPK) for b in range(PEER_TOPK)
               if (a + 1) * (b + 1) <= PEER_TOPK)


def _dot_t(a, b):
    return lax.dot_general(a, b, (((1,), (1,)), ((), ())),
                           preferred_element_type=F32)


def _group_mean_sq(y, bd):
    y2 = y * y
    hi = y2.astype(BF16)
    lo = (y2 - hi.astype(F32)).astype(BF16)
    outs = []
    for j in range(y.shape[1] // MXU_DIM):
        sl = slice(MXU_DIM * j, MXU_DIM * (j + 1))
        outs.append(jnp.dot(hi[:, sl], bd, preferred_element_type=F32)
                    + jnp.dot(lo[:, sl], bd, preferred_element_type=F32))
    ss = outs[0] if len(outs) == 1 else jnp.concatenate(outs, axis=1)
    return ss * (1.0 / HEAD_DIM)


def _rope(t, rc, rs1, rs2):
    outs = []
    for j in range(t.shape[1] // LANES):
        tj = t[:, LANES * j:LANES * (j + 1)]
        outs.append(tj * rc
                    + pltpu.roll(tj, LANES - ROPE_DIM // 2, 1) * rs1
                    + pltpu.roll(tj, ROPE_DIM // 2, 1) * rs2)
    return outs[0] if len(outs) == 1 else jnp.concatenate(outs, axis=1)


def _ada_kernel(c_ref, w_ref, b_ref, o_ref):
    c = c_ref[...]
    s = c * jax.nn.sigmoid(c)
    o_ref[...] = jnp.dot(s, w_ref[...], preferred_element_type=F32,
                         precision=lax.Precision.HIGHEST) + b_ref[...]


def _ada(c, w_ada, b_ada):
    b, d = c.shape
    n = w_ada.shape[1]
    return pl.pallas_call(
        _ada_kernel,
        grid=(n // ADA_COLS,),
        in_specs=[pl.BlockSpec((b, d), lambda j: (0, 0)),
                  pl.BlockSpec((d, ADA_COLS), lambda j: (0, j)),
                  pl.BlockSpec((1, ADA_COLS), lambda j: (0, j))],
        out_specs=pl.BlockSpec((b, ADA_COLS), lambda j: (0, j)),
        out_shape=jax.ShapeDtypeStruct((b, n), F32),
        compiler_params=pltpu.CompilerParams(
            dimension_semantics=("arbitrary",), vmem_limit_bytes=VMEM_LIMIT),
        name="ada",
    )(c, w_ada, b_ada.reshape(1, n))


def _mixer_kernel(sinks_ref, x_ref, mod_ref, g1_ref, win_ref, gq_ref, gk_ref,
                  rc_ref, rs1_ref, rs2_ref, cw_ref, goa_ref, goc_ref, wout_ref,
                  bd_ref, o_ref, kk_ref, vv_ref, u_ref, qn_ref, mixed_ref):
    ts = x_ref.shape[0]
    si = pl.program_id(1)
    x = x_ref[...]
    mod = mod_ref[...]
    sh1, sc1, gt1 = mod[0:1], mod[1:2], mod[2:3]
    bd = bd_ref[...]
    rc, rs1, rs2 = rc_ref[...], rs1_ref[...], rs2_ref[...]

    ms = jnp.mean(x * x, axis=-1, keepdims=True)
    h1 = ((x * lax.rsqrt(ms + EPS)) * g1_ref[...]) * (1.0 + sc1) + sh1
    h1b = h1.astype(BF16)

    @pl.when(si == 0)
    def _():
        kk_ref[:, :, 0:WINDOW, :] = jnp.zeros((N_KV_HEADS, 2, WINDOW, LANES), BF16)
        vv_ref[:, :, 0:WINDOW, :] = jnp.zeros((N_KV_HEADS, 2, WINDOW, LANES), BF16)
        u_ref[0:8, :] = jnp.zeros((8, CONV_WIDTH), F32)

    @pl.when(si > 0)
    def _():
        kk_ref[:, :, 0:WINDOW, :] = kk_ref[:, :, ts:ts + WINDOW, :]
        vv_ref[:, :, 0:WINDOW, :] = vv_ref[:, :, ts:ts + WINDOW, :]
        u_ref[0:8, :] = u_ref[ts:ts + 8, :]

    def proj(lo, width):
        return jnp.dot(h1b, win_ref[:, lo:lo + width], preferred_element_type=F32)

    q = proj(0, ATTN_WIDTH)
    q = q * lax.rsqrt(_group_mean_sq(q, bd) + EPS) * gq_ref[...]
    qn_ref[...] = (_rope(q, rc, rs1, rs2) * ATTN_SCALE).astype(BF16)

    k = proj(ATTN_WIDTH, KV_WIDTH)
    k = k * lax.rsqrt(_group_mean_sq(k, bd) + EPS) * gk_ref[...]
    k = _rope(k, rc, rs1, rs2)
    v = proj(ATTN_WIDTH + KV_WIDTH, KV_WIDTH)
    low = lax.broadcasted_iota(jnp.int32, (1, LANES), 1) < HEAD_DIM
    for g in range(N_KV_HEADS):
        col = slice(LANES * (g // 2), LANES * (g // 2 + 1))
        for t, dst in ((k[:, col], kk_ref), (v[:, col], vv_ref)):
            tr = pltpu.roll(t, HEAD_DIM, 1)
            in_low, in_high = (t, tr) if g % 2 == 0 else (tr, t)
            dst[g, 0, WINDOW:WINDOW + ts, :] = jnp.where(low, in_low, 0.0).astype(BF16)
            dst[g, 1, WINDOW:WINDOW + ts, :] = jnp.where(low, 0.0, in_high).astype(BF16)

    qi = lax.broadcasted_iota(jnp.int32, (WINDOW, 2 * WINDOW), 0)
    kj = lax.broadcasted_iota(jnp.int32, (WINDOW, 2 * WINDOW), 1)
    band = (kj > qi) & (kj <= qi + WINDOW)
    for jb in range(ts // WINDOW):
        first_key = jnp.where(si * (ts // WINDOW) + jb == 0, WINDOW, 0)
        mask = band & (kj >= first_key)
        qrows = slice(jb * WINDOW, (jb + 1) * WINDOW)
        krows = slice(jb * WINDOW, (jb + 2) * WINDOW)
        for g in range(N_KV_HEADS):
            qa = qn_ref[qrows, MXU_DIM * g:MXU_DIM * g + LANES]
            qb = qn_ref[qrows, MXU_DIM * g + LANES:MXU_DIM * (g + 1)]
            q2 = jnp.concatenate([qa, qb], axis=0)
            s_lo = _dot_t(q2, kk_ref[g, 0, krows, :])
            s_hi = _dot_t(q2, kk_ref[g, 1, krows, :])
            vcat = jnp.concatenate([vv_ref[g, 0, krows, :], vv_ref[g, 1, krows, :]], axis=0)
            halves = []
            for half in range(2):
                rows = slice(half * WINDOW, (half + 1) * WINDOW)
                ps, inv = [], []
                for o, s in enumerate((s_lo[rows], s_hi[rows])):
                    sink = sinks_ref[Q_PER_KV * g + 2 * half + o]
                    s = jnp.where(mask, s, NEG_INF)
                    m = jnp.maximum(jnp.max(s, axis=-1, keepdims=True), sink)
                    p = jnp.exp(s - m)
                    denom = jnp.sum(p, axis=-1, keepdims=True) + jnp.exp(sink - m)
                    ps.append(p.astype(BF16))
                    inv.append(1.0 / denom)
                pv = jnp.dot(jnp.concatenate(ps, axis=1), vcat, preferred_element_type=F32)
                halves.append(pv * jnp.where(low, inv[0], inv[1]))
            att = jnp.concatenate(halves, axis=1)
            cols = slice(MXU_DIM * g, MXU_DIM * (g + 1))
            att = att * lax.rsqrt(_group_mean_sq(att, bd) + EPS) * goa_ref[:, cols]
            mixed_ref[qrows, cols] = att.astype(BF16)

    o0 = ATTN_WIDTH + 2 * KV_WIDTH
    bg = proj(o0, CONV_WIDTH)
    u_ref[8:8 + ts, :] = proj(o0 + CONV_WIDTH, CONV_WIDTH) * proj(o0 + 2 * CONV_WIDTH, CONV_WIDTH)
    cw = cw_ref[...]
    acc = cw[0:1] * u_ref[6:6 + ts, :]
    acc = acc + cw[1:2] * u_ref[7:7 + ts, :]
    acc = acc + cw[2:3] * u_ref[8:8 + ts, :]
    y = bg * acc
    y = y * lax.rsqrt(_group_mean_sq(y, bd) + EPS) * goc_ref[...]
    mixed_ref[:, ATTN_WIDTH:] = y.astype(BF16)

    o_ref[...] = x + gt1 * jnp.dot(mixed_ref[...], wout_ref[...], preferred_element_type=F32)


def _mixer(x, mod, g_norm1, w_in_b, gq, gk, rc, rs1, rs2, conv_w, goa, goc, w_out_b, bd, sinks):
    b, s, d = x.shape
    ts = min(MIX_ROWS, s)
    const = lambda shape: pl.BlockSpec(shape, lambda bi, si: (0,) * len(shape),
                                       pipeline_mode=pl.Buffered(1))
    rope_spec = pl.BlockSpec((ts, LANES), lambda bi, si: (si, 0))
    return pl.pallas_call(
        _mixer_kernel,
        grid=(b, s // ts),
        in_specs=[pl.BlockSpec(memory_space=pltpu.SMEM),
                  pl.BlockSpec((None, ts, d), lambda bi, si: (bi, si, 0)),
                  pl.BlockSpec((None, 6, d), lambda bi, si: (bi, 0, 0)),
                  const((1, d)),
                  const((d, IN_COLS)),
                  const((1, ATTN_WIDTH)),
                  const((1, KV_WIDTH)),
                  rope_spec, rope_spec, rope_spec,
                  const((CONV_K, CONV_WIDTH)),
                  const((1, ATTN_WIDTH)),
                  const((1, CONV_WIDTH)),
                  const((d, d)),
                  const((MXU_DIM, MXU_DIM))],
        out_specs=pl.BlockSpec((None, ts, d), lambda bi, si: (bi, si, 0)),
        out_shape=jax.ShapeDtypeStruct((b, s, d), F32),
        scratch_shapes=[pltpu.VMEM((N_KV_HEADS, 2, ts + WINDOW, LANES), BF16),
                        pltpu.VMEM((N_KV_HEADS, 2, ts + WINDOW, LANES), BF16),
                        pltpu.VMEM((ts + 8, CONV_WIDTH), F32),
                        pltpu.VMEM((ts, ATTN_WIDTH), BF16),
                        pltpu.VMEM((ts, d), BF16)],
        compiler_params=pltpu.CompilerParams(
            dimension_semantics=("arbitrary", "arbitrary"), vmem_limit_bytes=VMEM_LIMIT),
        name="mixer",
    )(sinks, x, mod, g_norm1, w_in_b, gq, gk, rc, rs1, rs2, conv_w, goa, goc, w_out_b, bd)


def _fold_kernel(sk_ref, w_ref, o_ref):
    o_ref[...] = _dot_t(sk_ref[...], w_ref[...]).astype(BF16)


def _fold(subkeys_b, w_pq_b):
    d = w_pq_b.shape[0]
    return pl.pallas_call(
        _fold_kernel,
        grid=(2, PEER_HEADS),
        in_specs=[pl.BlockSpec((None, None, PEER_NKEYS, PEER_HALF), lambda p, h: (h, p, 0, 0)),
                  pl.BlockSpec((d, PEER_HALF), lambda p, h: (0, 2 * h + p))],
        out_specs=pl.BlockSpec((PEER_NKEYS, d), lambda p, h: (p * PEER_HEADS + h, 0)),
        out_shape=jax.ShapeDtypeStruct((2 * PEER_HEADS * PEER_NKEYS, d), BF16),
        compiler_params=pltpu.CompilerParams(
            dimension_semantics=("arbitrary", "arbitrary"), vmem_limit_bytes=VMEM_LIMIT),
        name="fold",
    )(subkeys_b, w_pq_b)


def _top16(s, want_rank=False):
    vals = []
    rank = jnp.full(s.shape, float(PEER_TOPK), F32) if want_rank else None
    for it in range(PEER_TOPK):
        m = jnp.max(s, axis=0, keepdims=True)
        vals.append(m)
        hit = s == m
        if want_rank:
            rank = jnp.where(hit, float(it), rank)
        if it + 1 < PEER_TOPK:
            s = jnp.where(hit, -jnp.inf, s)
    return (vals, rank) if want_rank else vals


def _route_kernel(x_ref, mod_ref, g2_ref, a_ref, h2_ref, rank2_ref, e2w_ref, cnt_ref, e1w_ref, st_ref):
    x = x_ref[...]
    mod = mod_ref[...]
    sh2, sc2 = mod[3:4], mod[4:5]
    ms = jnp.mean(x * x, axis=-1, keepdims=True)
    h2 = ((x * lax.rsqrt(ms + EPS)) * g2_ref[...]) * (1.0 + sc2) + sh2
    h2b = h2.astype(BF16)
    h2_ref[...] = h2b
    st_ref[...] = _dot_t(a_ref[...], h2b)

    hk = PEER_HEADS * PEER_NKEYS
    for g in range(x.shape[0] // LANES):
        lanes = slice(g * LANES, (g + 1) * LANES)
        t1, t2 = [], []
        for h in range(PEER_HEADS):
            t1.append(_top16(st_ref[h * PEER_NKEYS:(h + 1) * PEER_NKEYS, lanes]))
            vals, rank = _top16(st_ref[hk + h * PEER_NKEYS:hk + (h + 1) * PEER_NKEYS, lanes], True)
            t2.append(vals)
            rank2_ref[g, h] = rank.astype(BF16)
        r1 = [jnp.concatenate([t1[h][a] for h in range(PEER_HEADS)], axis=0) for a in range(PEER_TOPK)]
        r2 = [jnp.concatenate([t2[h][a] for h in range(PEER_HEADS)], axis=0) for a in range(PEER_TOPK)]
        zs = [r1[a] + r2[b] for (a, b) in _PAIRS]
        cur = list(zs)
        z16 = None
        for it in range(PEER_TOPK):
            z16 = functools.reduce(jnp.maximum, cur)
            if it + 1 < PEER_TOPK:
                cur = [jnp.where(c == z16, -jnp.inf, c) for c in cur]
        zmax = zs[0]
        zsum = jnp.zeros_like(zmax)
        cnt = [jnp.zeros_like(zmax) for _ in range(PEER_TOPK)]
        for (a, b), z in zip(_PAIRS, zs):
            sel = z >= z16
            zsum = zsum + jnp.where(sel, jnp.exp(z - zmax), 0.0)
            cnt[a] = cnt[a] + jnp.where(sel, 1.0, 0.0)
        inv_z = 1.0 / zsum
        for h in range(PEER_HEADS):
            s1 = st_ref[h * PEER_NKEYS:(h + 1) * PEER_NKEYS, lanes]
            s2 = st_ref[hk + h * PEER_NKEYS:hk + (h + 1) * PEER_NKEYS, lanes]
            cnt_full = jnp.zeros_like(s1)
            for a in range(PEER_TOPK):
                cnt_full = jnp.where(s1 == t1[h][a], cnt[a][h:h + 1, :], cnt_full)
            e2w_ref[g, h] = (jnp.exp(s2 - t2[h][0]) * inv_z[h:h + 1, :]).astype(BF16)
            cnt_ref[g, h] = cnt_full
            e1w_ref[g, h] = jnp.exp(s1 - t1[h][0])


def _route(x1, mod, g_norm2, a_mat):
    b, s, d = x1.shape
    n = b * s
    tr = min(ROUTE_ROWS, s)
    per_seq = s // tr
    grp = tr // LANES
    rshape = (n // LANES, PEER_HEADS, PEER_NKEYS, LANES)
    rspec = pl.BlockSpec((grp, PEER_HEADS, PEER_NKEYS, LANES), lambda i: (i, 0, 0, 0))
    return pl.pallas_call(
        _route_kernel,
        grid=(n // tr,),
        in_specs=[pl.BlockSpec((None, tr, d), lambda i: (i // per_seq, i % per_seq, 0)),
                  pl.BlockSpec((None, 6, d), lambda i: (i // per_seq, 0, 0)),
                  pl.BlockSpec((1, d), lambda i: (0, 0), pipeline_mode=pl.Buffered(1)),
                  pl.BlockSpec(a_mat.shape, lambda i: (0, 0), pipeline_mode=pl.Buffered(1))],
        out_specs=[pl.BlockSpec((tr, d), lambda i: (i, 0)), rspec, rspec, rspec, rspec],
        out_shape=[jax.ShapeDtypeStruct((n, d), BF16),
                   jax.ShapeDtypeStruct(rshape, BF16), jax.ShapeDtypeStruct(rshape, BF16),
                   jax.ShapeDtypeStruct(rshape, F32), jax.ShapeDtypeStruct(rshape, F32)],
        scratch_shapes=[pltpu.VMEM((a_mat.shape[0], tr), F32)],
        compiler_params=pltpu.CompilerParams(
            dimension_semantics=("arbitrary",), vmem_limit_bytes=VMEM_LIMIT),
        name="route",
    )(x1, mod, g_norm2, a_mat)


def _peer_kernel(h2_ref, u_ref, vt_ref, rank2_ref, e2w_ref, cnt_ref, e1w_ref, yt_ref,
                 a0_scr, a1_scr, act0_scr, act1_scr):
    g = pl.program_id(0)
    groups, ec = a0_scr.shape[0], a0_scr.shape[1]
    d = yt_ref.shape[0]
    n_chunks = PEER_NKEYS // PEER_E1

    @pl.when(g == 0)
    def _():
        for ref in (a0_scr, a1_scr, act0_scr, act1_scr):
            ref[...] = jnp.zeros(ref.shape, ref.dtype)

    @pl.when((g == 0) | ((g >= 2) & ((g - 2) % n_chunks == 0)))
    def _():
        yt_ref[...] = jnp.zeros(yt_ref.shape, F32)

    sqrt_half = math.sqrt(0.5)

    def run(a_new, a_old, act_new, act_old):
        nblk = PEER_NKEYS // E2_ROWS

        def gate_thunks(e1l, tb):
            w = [jnp.zeros((E2_ROWS, LANES), BF16) for _ in range(nblk)]

            def head_part(h):
                cnt = jnp.broadcast_to(cnt_ref[tb, h, e1l:e1l + 1, :], (E2_ROWS, LANES)).astype(BF16)
                w1 = jnp.broadcast_to(e1w_ref[tb, h, e1l:e1l + 1, :], (E2_ROWS, LANES)).astype(BF16)
                for eb in range(nblk):
                    e2rows = slice(eb * E2_ROWS, (eb + 1) * E2_ROWS)
                    w[eb] = w[eb] + w1 * jnp.where(rank2_ref[tb, h, e2rows, :] < cnt,
                                                   e2w_ref[tb, h, e2rows, :], jnp.zeros((), BF16))

            def finish_part(eb):
                rows = slice(e1l * PEER_NKEYS + eb * E2_ROWS,
                             e1l * PEER_NKEYS + (eb + 1) * E2_ROWS)
                a = a_old[tb, rows, :]
                gelu = 0.5 * a * (1.0 + lax.erf(a * sqrt_half))
                lanes = slice((tb % 2) * LANES, (tb % 2 + 1) * LANES)
                act_new[tb // 2, rows, lanes] = gelu.astype(BF16) * w[eb]

            return ([functools.partial(head_part, h) for h in range(PEER_HEADS)]
                    + [functools.partial(finish_part, eb) for eb in range(nblk)])

        thunks = [t for e1l in range(PEER_E1) for tb in range(groups) for t in gate_thunks(e1l, tb)]

        halves = groups // 2
        kt_score, kt_value = d // MXU_DIM, ec // MXU_DIM
        score_rows, value_rows = min(SCORE_ROWS, ec), min(VALUE_ROWS, d)
        mxu_ops = [("score", rb, nb) for rb in range(ec // score_rows) for nb in range(halves)]
        mxu_ops += [("value", rb, nb) for rb in range(d // value_rows) for nb in range(halves)]
        n_small = halves * ((ec // MXU_DIM) * kt_score + (d // MXU_DIM) * kt_value)
        done_small, done_thunks = 0, 0

        def vpu_fill():
            nonlocal done_thunks
            target = -(-len(thunks) * done_small // n_small)
            while done_thunks < min(target, len(thunks)):
                thunks[done_thunks]()
                done_thunks += 1

        for kind, rb, nb in mxu_ops:
            tok = slice(nb * MXU_DIM, (nb + 1) * MXU_DIM)
            acc = None
            if kind == "score":
                rows = slice(rb * score_rows, (rb + 1) * score_rows)
                for k in range(kt_score):
                    ks = slice(k * MXU_DIM, (k + 1) * MXU_DIM)
                    part = _dot_t(u_ref[rows, ks], h2_ref[tok, ks])
                    acc = part if acc is None else acc + part
                    done_small += score_rows // MXU_DIM
                    vpu_fill()
                a_new[2 * nb, rows, :] = acc[:, :LANES]
                a_new[2 * nb + 1, rows, :] = acc[:, LANES:]
            else:
                rows = slice(rb * value_rows, (rb + 1) * value_rows)
                for k in range(kt_value):
                    ks = slice(k * MXU_DIM, (k + 1) * MXU_DIM)
                    part = jnp.dot(vt_ref[rows, ks], act_old[nb, ks, :], preferred_element_type=F32)
                    acc = part if acc is None else acc + part
                    done_small += value_rows // MXU_DIM
                    vpu_fill()
                yt_ref[rows, tok] += acc
        assert done_thunks == len(thunks)

    @pl.when(g % 2 == 0)
    def _():
        run(a0_scr, a1_scr, act1_scr, act0_scr)

    @pl.when(g % 2 == 1)
    def _():
        run(a1_scr, a0_scr, act0_scr, act1_scr)


def _peer(h2, u_b, vt_b, rank2, e2w, cnt, e1w):
    n, d = h2.shape
    tp = min(PEER_ROWS, n)
    grp = tp // LANES
    ec = PEER_E1 * PEER_NKEYS
    nc = PEER_N // ec
    total = (n // tp) * nc
    lag = lambda g, k: jnp.clip(g - k, 0, total - 1)
    full = pl.BlockSpec((grp, PEER_HEADS, PEER_NKEYS, LANES), lambda g: (lag(g, 1) // nc, 0, 0, 0))
    part = pl.BlockSpec((grp, PEER_HEADS, PEER_E1, LANES),
                        lambda g: (lag(g, 1) // nc, 0, lag(g, 1) % nc, 0))
    return pl.pallas_call(
        _peer_kernel,
        grid=(total + 2,),
        in_specs=[pl.BlockSpec((tp, d), lambda g: (lag(g, 0) // nc, 0)),
                  pl.BlockSpec((ec, d), lambda g: (lag(g, 0) % nc, 0)),
                  pl.BlockSpec((d, ec), lambda g: (0, lag(g, 2) % nc)),
                  full, full, part, part],
        out_specs=pl.BlockSpec((d, tp), lambda g: (0, lag(g, 2) // nc)),
        out_shape=jax.ShapeDtypeStruct((d, n), F32),
        scratch_shapes=[pltpu.VMEM((grp, ec, LANES), F32), pltpu.VMEM((grp, ec, LANES), F32),
                        pltpu.VMEM((grp // 2, ec, MXU_DIM), BF16),
                        pltpu.VMEM((grp // 2, ec, MXU_DIM), BF16)],
        compiler_params=pltpu.CompilerParams(
            dimension_semantics=("arbitrary",), vmem_limit_bytes=VMEM_LIMIT),
        name="peer",
    )(h2, u_b, vt_b, rank2, e2w, cnt, e1w)


def _final_kernel(x_ref, mod_ref, yt_ref, o_ref):
    gt2 = mod_ref[...][5:6]
    o_ref[...] = x_ref[...] + gt2 * yt_ref[...].T


def _final(x1, mod, yt):
    b, s, d = x1.shape
    tf = min(PEER_ROWS, s)
    per_seq = s // tf
    return pl.pallas_call(
        _final_kernel,
        grid=(b, per_seq),
        in_specs=[pl.BlockSpec((None, tf, d), lambda bi, si: (bi, si, 0)),
                  pl.BlockSpec((None, 6, d), lambda bi, si: (bi, 0, 0)),
                  pl.BlockSpec((d, tf), lambda bi, si: (0, bi * per_seq + si))],
        out_specs=pl.BlockSpec((None, tf, d), lambda bi, si: (bi, si, 0)),
        out_shape=jax.ShapeDtypeStruct((b, s, d), F32),
        compiler_params=pltpu.CompilerParams(
            dimension_semantics=("arbitrary", "arbitrary"), vmem_limit_bytes=VMEM_LIMIT),
        name="final",
    )(x1, mod, yt)


def _rope_tables(s):
    pos = jnp.arange(s, dtype=F32)
    inv_freq = ROPE_THETA ** (-jnp.arange(0, ROPE_DIM, 2, dtype=F32) / ROPE_DIM)
    ang = pos[:, None] * inv_freq[None, :]
    cos, sin = jnp.cos(ang), jnp.sin(ang)
    half = ROPE_DIM // 2
    ones = jnp.ones((s, HEAD_DIM - ROPE_DIM), F32)
    zeros = jnp.zeros((s, HEAD_DIM - ROPE_DIM), F32)
    zh = jnp.zeros((s, half), F32)
    rc = jnp.concatenate([cos, cos, ones], axis=1)
    rs1 = jnp.concatenate([-sin, zh, zeros], axis=1)
    rs2 = jnp.concatenate([zh, sin, zeros], axis=1)
    tile = lambda t: jnp.tile(t, (1, LANES // HEAD_DIM))
    return tile(rc), tile(rs1), tile(rs2)


def kernel(x, c, w_ada, b_ada, g_norm1, w_in, g_q, g_k, sinks, conv_w, g_out_attn,
           g_out_conv, w_out, g_norm2, w_pq, peer_subkeys, peer_u, peer_v):
    b, s, d = x.shape
    assert d == D_MODEL and s % WINDOW == 0 and (b * s) % LANES == 0

    mod = _ada(c, w_ada, b_ada).reshape(b, 6, d)

    rc, rs1, rs2 = _rope_tables(s)
    blk = jnp.arange(MXU_DIM) // HEAD_DIM
    bd = (blk[:, None] == blk[None, :]).astype(BF16)
    x1 = _mixer(x, mod, g_norm1.reshape(1, d), w_in.astype(BF16),
                jnp.tile(g_q, N_Q_HEADS).reshape(1, ATTN_WIDTH),
                jnp.tile(g_k, N_KV_HEADS).reshape(1, KV_WIDTH),
                rc, rs1, rs2, conv_w, g_out_attn.reshape(1, ATTN_WIDTH),
                g_out_conv.reshape(1, CONV_WIDTH), w_out.astype(BF16), bd, sinks)

    a_mat = _fold(peer_subkeys.astype(BF16), w_pq.astype(BF16))
    h2, rank2, e2w, cnt, e1w = _route(x1, mod, g_norm2.reshape(1, d), a_mat)
    yt = _peer(h2, peer_u.astype(BF16), peer_v.T.astype(BF16), rank2, e2w, cnt, e1w)
    return _final(x1, mod, yt)
```

```python
import functools
import math

import jax
import jax.numpy as jnp
from jax import lax
from jax.experimental import pallas as pl
from jax.experimental.pallas import tpu as pltpu

F32 = jnp.float32
BF16 = jnp.bfloat16

D_MODEL = 2048
HEAD_DIM = 64
N_Q_HEADS = 16
N_KV_HEADS = 4
Q_PER_KV = N_Q_HEADS // N_KV_HEADS
ATTN_WIDTH = N_Q_HEADS * HEAD_DIM
KV_WIDTH = N_KV_HEADS * HEAD_DIM
CONV_WIDTH = D_MODEL - ATTN_WIDTH
CONV_K = 3
IN_COLS = ATTN_WIDTH + 2 * KV_WIDTH + 3 * CONV_WIDTH
WINDOW = 128
ROPE_THETA = 500000.0
ROPE_DIM = HEAD_DIM // 4
ATTN_SCALE = HEAD_DIM ** -0.5
NEG_INF = -1e30
PEER_HEADS = 8
PEER_NKEYS = 128
PEER_N = PEER_NKEYS * PEER_NKEYS
PEER_HALF = 128
PEER_TOPK = 16
EPS = 1e-6

LANES = 128
MXU_DIM = 256
VMEM_LIMIT = 56 * 1024 * 1024

ADA_COLS = 1024
MIX_ROWS = 256
ROUTE_ROWS = 256
PEER_ROWS = 512
PEER_E1 = 8
E2_ROWS = 32
SCORE_ROWS = 512
VALUE_ROWS = 512

_PAIRS = tuple((a, b) for a in range(PEER_TOPK) for b in range(PEER_TOPK)
               if (a + 1) * (b + 1) <= PEER_TOPK)


def _dot_t(a, b):
    return lax.dot_general(a, b, (((1,), (1,)), ((), ())),
                           preferred_element_type=F32)


def _group_mean_sq(y, bd):
    y2 = y * y
    hi = y2.astype(BF16)
    lo = (y2 - hi.astype(F32)).astype(BF16)
    outs = []
    for j in range(y.shape[1] // MXU_DIM):
        sl = slice(MXU_DIM * j, MXU_DIM * (j + 1))
        outs.append(jnp.dot(hi[:, sl], bd, preferred_element_type=F32)
                    + jnp.dot(lo[:, sl], bd, preferred_element_type=F32))
    ss = outs[0] if len(outs) == 1 else jnp.concatenate(outs, axis=1)
    return ss * (1.0 / HEAD_DIM)


def _rope(t, rc, rs1, rs2):
    outs = []
    for j in range(t.shape[1] // LANES):
        tj = t[:, LANES * j:LANES * (j + 1)]
        outs.append(tj * rc
                    + pltpu.roll(tj, LANES - ROPE_DIM // 2, 1) * rs1
                    + pltpu.roll(tj, ROPE_DIM // 2, 1) * rs2)
    return outs[0] if len(outs) == 1 else jnp.concatenate(outs, axis=1)


def _ada_kernel(c_ref, w_ref, b_ref, o_ref):
    c = c_ref[...]
    s = c * jax.nn.sigmoid(c)
    o_ref[...] = jnp.dot(s, w_ref[...], preferred_element_type=F32,
                         precision=lax.Precision.HIGHEST) + b_ref[...]


def _ada(c, w_ada, b_ada):
    b, d = c.shape
    n = w_ada.shape[1]
    return pl.pallas_call(
        _ada_kernel,
        grid=(n // ADA_COLS,),
        in_specs=[pl.BlockSpec((b, d), lambda j: (0, 0)),
                  pl.BlockSpec((d, ADA_COLS), lambda j: (0, j)),
                  pl.BlockSpec((1, ADA_COLS), lambda j: (0, j))],
        out_specs=pl.BlockSpec((b, ADA_COLS), lambda j: (0, j)),
        out_shape=jax.ShapeDtypeStruct((b, n), F32),
        compiler_params=pltpu.CompilerParams(
            dimension_semantics=("arbitrary",), vmem_limit_bytes=VMEM_LIMIT),
        name="ada",
    )(c, w_ada, b_ada.reshape(1, n))


def _mixer_kernel(sinks_ref, x_ref, mod_ref, g1_ref, win_ref, gq_ref, gk_ref,
                  rc_ref, rs1_ref, rs2_ref, cw_ref, goa_ref, goc_ref, wout_ref,
                  bd_ref, o_ref, kk_ref, vv_ref, u_ref, qn_ref, mixed_ref):
    ts = x_ref.shape[0]
    si = pl.program_id(1)
    x = x_ref[...]
    mod = mod_ref[...]
    sh1, sc1, gt1 = mod[0:1], mod[1:2], mod[2:3]
    bd = bd_ref[...]
    rc, rs1, rs2 = rc_ref[...], rs1_ref[...], rs2_ref[...]

    ms = jnp.mean(x * x, axis=-1, keepdims=True)
    h1 = ((x * lax.rsqrt(ms + EPS)) * g1_ref[...]) * (1.0 + sc1) + sh1
    h1b = h1.astype(BF16)

    @pl.when(si == 0)
    def _():
        kk_ref[:, :, 0:WINDOW, :] = jnp.zeros((N_KV_HEADS, 2, WINDOW, LANES), BF16)
        vv_ref[:, :, 0:WINDOW, :] = jnp.zeros((N_KV_HEADS, 2, WINDOW, LANES), BF16)
        u_ref[0:8, :] = jnp.zeros((8, CONV_WIDTH), F32)

    @pl.when(si > 0)
    def _():
        kk_ref[:, :, 0:WINDOW, :] = kk_ref[:, :, ts:ts + WINDOW, :]
        vv_ref[:, :, 0:WINDOW, :] = vv_ref[:, :, ts:ts + WINDOW, :]
        u_ref[0:8, :] = u_ref[ts:ts + 8, :]

    def proj(lo, width):
        return jnp.dot(h1b, win_ref[:, lo:lo + width], preferred_element_type=F32)

    q = proj(0, ATTN_WIDTH)
    q = q * lax.rsqrt(_group_mean_sq(q, bd) + EPS) * gq_ref[...]
    qn_ref[...] = (_rope(q, rc, rs1, rs2) * ATTN_SCALE).astype(BF16)

    k = proj(ATTN_WIDTH, KV_WIDTH)
    k = k * lax.rsqrt(_group_mean_sq(k, bd) + EPS) * gk_ref[...]
    k = _rope(k, rc, rs1, rs2)
    v = proj(ATTN_WIDTH + KV_WIDTH, KV_WIDTH)
    low = lax.broadcasted_iota(jnp.int32, (1, LANES), 1) < HEAD_DIM
    for g in range(N_KV_HEADS):
        col = slice(LANES * (g // 2), LANES * (g // 2 + 1))
        for t, dst in ((k[:, col], kk_ref), (v[:, col], vv_ref)):
            tr = pltpu.roll(t, HEAD_DIM, 1)
            in_low, in_high = (t, tr) if g % 2 == 0 else (tr, t)
            dst[g, 0, WINDOW:WINDOW + ts, :] = jnp.where(low, in_low, 0.0).astype(BF16)
            dst[g, 1, WINDOW:WINDOW + ts, :] = jnp.where(low, 0.0, in_high).astype(BF16)

    qi = lax.broadcasted_iota(jnp.int32, (WINDOW, 2 * WINDOW), 0)
    kj = lax.broadcasted_iota(jnp.int32, (WINDOW, 2 * WINDOW), 1)
    band = (kj > qi) & (kj <= qi + WINDOW)

    def scores(jb, g):
        qrows = slice(jb * WINDOW, (jb + 1) * WINDOW)
        krows = slice(jb * WINDOW, (jb + 2) * WINDOW)
        qa = qn_ref[qrows, MXU_DIM * g:MXU_DIM * g + LANES]
        qb = qn_ref[qrows, MXU_DIM * g + LANES:MXU_DIM * (g + 1)]
        q2 = jnp.concatenate([qa, qb], axis=0)
        return (_dot_t(q2, kk_ref[g, 0, krows, :]),
                _dot_t(q2, kk_ref[g, 1, krows, :]))

    def attend(jb, g, s_lo, s_hi):
        first_key = jnp.where(si * (ts // WINDOW) + jb == 0, WINDOW, 0)
        mask = band & (kj >= first_key)
        qrows = slice(jb * WINDOW, (jb + 1) * WINDOW)
        krows = slice(jb * WINDOW, (jb + 2) * WINDOW)
        vcat = jnp.concatenate([vv_ref[g, 0, krows, :], vv_ref[g, 1, krows, :]], axis=0)
        halves = []
        for half in range(2):
            rows = slice(half * WINDOW, (half + 1) * WINDOW)
            ps, inv = [], []
            for o, s in enumerate((s_lo[rows], s_hi[rows])):
                sink = sinks_ref[Q_PER_KV * g + 2 * half + o]
                s = jnp.where(mask, s, NEG_INF)
                m = jnp.maximum(jnp.max(s, axis=-1, keepdims=True), sink)
                p = jnp.exp(s - m)
                denom = jnp.sum(p, axis=-1, keepdims=True) + jnp.exp(sink - m)
                ps.append(p.astype(BF16))
                inv.append(1.0 / denom)
            pv = jnp.dot(jnp.concatenate(ps, axis=1), vcat, preferred_element_type=F32)
            halves.append(pv * jnp.where(low, inv[0], inv[1]))
        att = jnp.concatenate(halves, axis=1)
        cols = slice(MXU_DIM * g, MXU_DIM * (g + 1))
        att = att * lax.rsqrt(_group_mean_sq(att, bd) + EPS) * goa_ref[:, cols]
        mixed_ref[qrows, cols] = att.astype(BF16)

    o0 = ATTN_WIDTH + 2 * KV_WIDTH
    cw = cw_ref[...]

    def conv_block(cb):
        cols = slice(cb * MXU_DIM, (cb + 1) * MXU_DIM)
        bg = proj(o0 + cb * MXU_DIM, MXU_DIM)
        u_ref[8:8 + ts, cols] = (proj(o0 + CONV_WIDTH + cb * MXU_DIM, MXU_DIM)
                                 * proj(o0 + 2 * CONV_WIDTH + cb * MXU_DIM, MXU_DIM))
        acc = cw[0:1, cols] * u_ref[6:6 + ts, cols]
        acc = acc + cw[1:2, cols] * u_ref[7:7 + ts, cols]
        acc = acc + cw[2:3, cols] * u_ref[8:8 + ts, cols]
        y = bg * acc
        y = y * lax.rsqrt(_group_mean_sq(y, bd) + EPS) * goc_ref[:, cols]
        mixed_ref[:, ATTN_WIDTH + cb * MXU_DIM:ATTN_WIDTH + (cb + 1) * MXU_DIM] = y.astype(BF16)

    iters = [(jb, g) for jb in range(ts // WINDOW) for g in range(N_KV_HEADS)]
    n_conv = CONV_WIDTH // MXU_DIM
    conv_done = 0
    s_cur = scores(*iters[0])
    for i, (jb, g) in enumerate(iters):
        s_next = scores(*iters[i + 1]) if i + 1 < len(iters) else None
        want = -(-n_conv * (i + 1) // len(iters))
        while conv_done < want:
            conv_block(conv_done)
            conv_done += 1
        attend(jb, g, *s_cur)
        s_cur = s_next

    o_ref[...] = x + gt1 * jnp.dot(mixed_ref[...], wout_ref[...], preferred_element_type=F32)


def _mixer(x, mod, g_norm1, w_in_b, gq, gk, rc, rs1, rs2, conv_w, goa, goc, w_out_b, bd, sinks):
    b, s, d = x.shape
    ts = min(MIX_ROWS, s)
    const = lambda shape: pl.BlockSpec(shape, lambda bi, si: (0,) * len(shape),
                                       pipeline_mode=pl.Buffered(1))
    rope_spec = pl.BlockSpec((ts, LANES), lambda bi, si: (si, 0))
    return pl.pallas_call(
        _mixer_kernel,
        grid=(b, s // ts),
        in_specs=[pl.BlockSpec(memory_space=pltpu.SMEM),
                  pl.BlockSpec((None, ts, d), lambda bi, si: (bi, si, 0)),
                  pl.BlockSpec((None, 6, d), lambda bi, si: (bi, 0, 0)),
                  const((1, d)),
                  const((d, IN_COLS)),
                  const((1, ATTN_WIDTH)),
                  const((1, KV_WIDTH)),
                  rope_spec, rope_spec, rope_spec,
                  const((CONV_K, CONV_WIDTH)),
                  const((1, ATTN_WIDTH)),
                  const((1, CONV_WIDTH)),
                  const((d, d)),
                  const((MXU_DIM, MXU_DIM))],
        out_specs=pl.BlockSpec((None, ts, d), lambda bi, si: (bi, si, 0)),
        out_shape=jax.ShapeDtypeStruct((b, s, d), F32),
        scratch_shapes=[pltpu.VMEM((N_KV_HEADS, 2, ts + WINDOW, LANES), BF16),
                        pltpu.VMEM((N_KV_HEADS, 2, ts + WINDOW, LANES), BF16),
                        pltpu.VMEM((ts + 8, CONV_WIDTH), F32),
                        pltpu.VMEM((ts, ATTN_WIDTH), BF16),
                        pltpu.VMEM((ts, d), BF16)],
        compiler_params=pltpu.CompilerParams(
            dimension_semantics=("arbitrary", "arbitrary"), vmem_limit_bytes=VMEM_LIMIT),
        name="mixer",
    )(sinks, x, mod, g_norm1, w_in_b, gq, gk, rc, rs1, rs2, conv_w, goa, goc, w_out_b, bd)


def _fold_kernel(sk_ref, w_ref, o_ref):
    o_ref[...] = _dot_t(sk_ref[...], w_ref[...]).astype(BF16)


def _fold(subkeys_b, w_pq_b):
    d = w_pq_b.shape[0]
    return pl.pallas_call(
        _fold_kernel,
        grid=(2, PEER_HEADS),
        in_specs=[pl.BlockSpec((None, None, PEER_NKEYS, PEER_HALF), lambda p, h: (h, p, 0, 0)),
                  pl.BlockSpec((d, PEER_HALF), lambda p, h: (0, 2 * h + p))],
        out_specs=pl.BlockSpec((PEER_NKEYS, d), lambda p, h: (p * PEER_HEADS + h, 0)),
        out_shape=jax.ShapeDtypeStruct((2 * PEER_HEADS * PEER_NKEYS, d), BF16),
        compiler_params=pltpu.CompilerParams(
            dimension_semantics=("arbitrary", "arbitrary"), vmem_limit_bytes=VMEM_LIMIT),
        name="fold",
    )(subkeys_b, w_pq_b)


def _top16(s, want_rank=False):
    vals = []
    rank = jnp.full(s.shape, float(PEER_TOPK), F32) if want_rank else None
    for it in range(PEER_TOPK):
        m = jnp.max(s, axis=0, keepdims=True)
        vals.append(m)
        hit = s == m
        if want_rank:
            rank = jnp.where(hit, float(it), rank)
        if it + 1 < PEER_TOPK:
            s = jnp.where(hit, -jnp.inf, s)
    return (vals, rank) if want_rank else vals


def _route_kernel(x_ref, mod_ref, g2_ref, a_ref, h2_ref, rank2_ref, e2w_ref, cnt_ref, e1w_ref, st_ref):
    x = x_ref[...]
    mod = mod_ref[...]
    sh2, sc2 = mod[3:4], mod[4:5]
    ms = jnp.mean(x * x, axis=-1, keepdims=True)
    h2 = ((x * lax.rsqrt(ms + EPS)) * g2_ref[...]) * (1.0 + sc2) + sh2
    h2b = h2.astype(BF16)
    h2_ref[...] = h2b
    st_ref[...] = _dot_t(a_ref[...], h2b)

    hk = PEER_HEADS * PEER_NKEYS
    for g in range(x.shape[0] // LANES):
        lanes = slice(g * LANES, (g + 1) * LANES)
        t1, t2 = [], []
        for h in range(PEER_HEADS):
            t1.append(_top16(st_ref[h * PEER_NKEYS:(h + 1) * PEER_NKEYS, lanes]))
            vals, rank = _top16(st_ref[hk + h * PEER_NKEYS:hk + (h + 1) * PEER_NKEYS, lanes], True)
            t2.append(vals)
            rank2_ref[g, h] = rank.astype(BF16)
        r1 = [jnp.concatenate([t1[h][a] for h in range(PEER_HEADS)], axis=0) for a in range(PEER_TOPK)]
        r2 = [jnp.concatenate([t2[h][a] for h in range(PEER_HEADS)], axis=0) for a in range(PEER_TOPK)]
        zs = [r1[a] + r2[b] for (a, b) in _PAIRS]
        cur = list(zs)
        z16 = None
        for it in range(PEER_TOPK):
            z16 = functools.reduce(jnp.maximum, cur)
            if it + 1 < PEER_TOPK:
                cur = [jnp.where(c == z16, -jnp.inf, c) for c in cur]
        zmax = zs[0]
        zsum = jnp.zeros_like(zmax)
        cnt = [jnp.zeros_like(zmax) for _ in range(PEER_TOPK)]
        for (a, b), z in zip(_PAIRS, zs):
            sel = z >= z16
            zsum = zsum + jnp.where(sel, jnp.exp(z - zmax), 0.0)
            cnt[a] = cnt[a] + jnp.where(sel, 1.0, 0.0)
        inv_z = 1.0 / zsum
        for h in range(PEER_HEADS):
            s1 = st_ref[h * PEER_NKEYS:(h + 1) * PEER_NKEYS, lanes]
            s2 = st_ref[hk + h * PEER_NKEYS:hk + (h + 1) * PEER_NKEYS, lanes]
            cnt_full = jnp.zeros_like(s1)
            for a in range(PEER_TOPK):
                cnt_full = jnp.where(s1 == t1[h][a], cnt[a][h:h + 1, :], cnt_full)
            e2w_ref[g, h] = (jnp.exp(s2 - t2[h][0]) * inv_z[h:h + 1, :]).astype(BF16)
            cnt_ref[g, h] = cnt_full
            e1w_ref[g, h] = jnp.exp(s1 - t1[h][0])


def _route(x1, mod, g_norm2, a_mat):
    b, s, d = x1.shape
    n = b * s
    tr = min(ROUTE_ROWS, s)
    per_seq = s // tr
    grp = tr // LANES
    rshape = (n // LANES, PEER_HEADS, PEER_NKEYS, LANES)
    rspec = pl.BlockSpec((grp, PEER_HEADS, PEER_NKEYS, LANES), lambda i: (i, 0, 0, 0))
    return pl.pallas_call(
        _route_kernel,
        grid=(n // tr,),
        in_specs=[pl.BlockSpec((None, tr, d), lambda i: (i // per_seq, i % per_seq, 0)),
                  pl.BlockSpec((None, 6, d), lambda i: (i // per_seq, 0, 0)),
                  pl.BlockSpec((1, d), lambda i: (0, 0), pipeline_mode=pl.Buffered(1)),
                  pl.BlockSpec(a_mat.shape, lambda i: (0, 0), pipeline_mode=pl.Buffered(1))],
        out_specs=[pl.BlockSpec((tr, d), lambda i: (i, 0)), rspec, rspec, rspec, rspec],
        out_shape=[jax.ShapeDtypeStruct((n, d), BF16),
                   jax.ShapeDtypeStruct(rshape, BF16), jax.ShapeDtypeStruct(rshape, BF16),
                   jax.ShapeDtypeStruct(rshape, F32), jax.ShapeDtypeStruct(rshape, F32)],
        scratch_shapes=[pltpu.VMEM((a_mat.shape[0], tr), F32)],
        compiler_params=pltpu.CompilerParams(
            dimension_semantics=("arbitrary",), vmem_limit_bytes=VMEM_LIMIT),
        name="route",
    )(x1, mod, g_norm2, a_mat)


def _peer_kernel(h2_ref, u_ref, vt_ref, rank2_ref, e2w_ref, cnt_ref, e1w_ref, yt_ref,
                 a0_scr, a1_scr, act0_scr, act1_scr):
    g = pl.program_id(0)
    groups, ec = a0_scr.shape[0], a0_scr.shape[1]
    d = yt_ref.shape[0]
    n_chunks = PEER_NKEYS // PEER_E1

    @pl.when(g == 0)
    def _():
        for ref in (a0_scr, a1_scr, act0_scr, act1_scr):
            ref[...] = jnp.zeros(ref.shape, ref.dtype)

    @pl.when((g == 0) | ((g >= 2) & ((g - 2) % n_chunks == 0)))
    def _():
        yt_ref[...] = jnp.zeros(yt_ref.shape, F32)

    sqrt_half = math.sqrt(0.5)

    def run(a_new, a_old, act_new, act_old):
        nblk = PEER_NKEYS // E2_ROWS

        def gate_thunks(e1l, tb):
            w = [jnp.zeros((E2_ROWS, LANES), BF16) for _ in range(nblk)]

            def head_part(h):
                cnt = jnp.broadcast_to(cnt_ref[tb, h, e1l:e1l + 1, :], (E2_ROWS, LANES)).astype(BF16)
                w1 = jnp.broadcast_to(e1w_ref[tb, h, e1l:e1l + 1, :], (E2_ROWS, LANES)).astype(BF16)
                for eb in range(nblk):
                    e2rows = slice(eb * E2_ROWS, (eb + 1) * E2_ROWS)
                    w[eb] = w[eb] + w1 * jnp.where(rank2_ref[tb, h, e2rows, :] < cnt,
                                                   e2w_ref[tb, h, e2rows, :], jnp.zeros((), BF16))

            def finish_part(eb):
                rows = slice(e1l * PEER_NKEYS + eb * E2_ROWS,
                             e1l * PEER_NKEYS + (eb + 1) * E2_ROWS)
                a = a_old[tb, rows, :]
                gelu = 0.5 * a * (1.0 + lax.erf(a * sqrt_half))
                lanes = slice((tb % 2) * LANES, (tb % 2 + 1) * LANES)
                act_new[tb // 2, rows, lanes] = gelu.astype(BF16) * w[eb]

            return ([functools.partial(head_part, h) for h in range(PEER_HEADS)]
                    + [functools.partial(finish_part, eb) for eb in range(nblk)])

        thunks = [t for e1l in range(PEER_E1) for tb in range(groups) for t in gate_thunks(e1l, tb)]

        halves = groups // 2
        kt_score, kt_value = d // MXU_DIM, ec // MXU_DIM
        score_rows, value_rows = min(SCORE_ROWS, ec), min(VALUE_ROWS, d)
        mxu_ops = [("score", rb, nb) for rb in range(ec // score_rows) for nb in range(halves)]
        mxu_ops += [("value", rb, nb) for rb in range(d // value_rows) for nb in range(halves)]
        n_small = halves * ((ec // MXU_DIM) * kt_score + (d // MXU_DIM) * kt_value)
        done_small, done_thunks = 0, 0

        def vpu_fill():
            nonlocal done_thunks
            target = -(-len(thunks) * done_small // n_small)
            while done_thunks < min(target, len(thunks)):
                thunks[done_thunks]()
                done_thunks += 1

        for kind, rb, nb in mxu_ops:
            tok = slice(nb * MXU_DIM, (nb + 1) * MXU_DIM)
            acc = None
            if kind == "score":
                rows = slice(rb * score_rows, (rb + 1) * score_rows)
                for k in range(kt_score):
                    ks = slice(k * MXU_DIM, (k + 1) * MXU_DIM)
                    part = _dot_t(u_ref[rows, ks], h2_ref[tok, ks])
                    acc = part if acc is None else acc + part
                    done_small += score_rows // MXU_DIM
                    vpu_fill()
                a_new[2 * nb, rows, :] = acc[:, :LANES]
                a_new[2 * nb + 1, rows, :] = acc[:, LANES:]
            else:
                rows = slice(rb * value_rows, (rb + 1) * value_rows)
                for k in range(kt_value):
                    ks = slice(k * MXU_DIM, (k + 1) * MXU_DIM)
                    part = jnp.dot(vt_ref[rows, ks], act_old[nb, ks, :], preferred_element_type=F32)
                    acc = part if acc is None else acc + part
                    done_small += value_rows // MXU_DIM
                    vpu_fill()
                yt_ref[rows, tok] += acc
        assert done_thunks == len(thunks)

    @pl.when(g % 2 == 0)
    def _():
        run(a0_scr, a1_scr, act1_scr, act0_scr)

    @pl.when(g % 2 == 1)
    def _():
        run(a1_scr, a0_scr, act0_scr, act1_scr)


def _peer(h2, u_b, vt_b, rank2, e2w, cnt, e1w):
    n, d = h2.shape
    tp = min(PEER_ROWS, n)
    grp = tp // LANES
    ec = PEER_E1 * PEER_NKEYS
    nc = PEER_N // ec
    total = (n // tp) * nc
    lag = lambda g, k: jnp.clip(g - k, 0, total - 1)
    full = pl.BlockSpec((grp, PEER_HEADS, PEER_NKEYS, LANES), lambda g: (lag(g, 1) // nc, 0, 0, 0))
    part = pl.BlockSpec((grp, PEER_HEADS, PEER_E1, LANES),
                        lambda g: (lag(g, 1) // nc, 0, lag(g, 1) % nc, 0))
    return pl.pallas_call(
        _peer_kernel,
        grid=(total + 2,),
        in_specs=[pl.BlockSpec((tp, d), lambda g: (lag(g, 0) // nc, 0)),
                  pl.BlockSpec((ec, d), lambda g: (lag(g, 0) % nc, 0)),
                  pl.BlockSpec((None, d, ec), lambda g: (lag(g, 2) % nc, 0, 0)),
                  full, full, part, part],
        out_specs=pl.BlockSpec((d, tp), lambda g: (0, lag(g, 2) // nc)),
        out_shape=jax.ShapeDtypeStruct((d, n), F32),
        scratch_shapes=[pltpu.VMEM((grp, ec, LANES), F32), pltpu.VMEM((grp, ec, LANES), F32),
                        pltpu.VMEM((grp // 2, ec, MXU_DIM), BF16),
                        pltpu.VMEM((grp // 2, ec, MXU_DIM), BF16)],
        compiler_params=pltpu.CompilerParams(
            dimension_semantics=("arbitrary",), vmem_limit_bytes=VMEM_LIMIT),
        name="peer",
    )(h2, u_b, vt_b, rank2, e2w, cnt, e1w)


def _final_kernel(x_ref, mod_ref, yt_ref, o_ref):
    gt2 = mod_ref[...][5:6]
    o_ref[...] = x_ref[...] + gt2 * yt_ref[...].T


def _final(x1, mod, yt):
    b, s, d = x1.shape
    tf = min(PEER_ROWS, s)
    per_seq = s // tf
    return pl.pallas_call(
        _final_kernel,
        grid=(b, per_seq),
        in_specs=[pl.BlockSpec((None, tf, d), lambda bi, si: (bi, si, 0)),
                  pl.BlockSpec((None, 6, d), lambda bi, si: (bi, 0, 0)),
                  pl.BlockSpec((d, tf), lambda bi, si: (0, bi * per_seq + si))],
        out_specs=pl.BlockSpec((None, tf, d), lambda bi, si: (bi, si, 0)),
        out_shape=jax.ShapeDtypeStruct((b, s, d), F32),
        compiler_params=pltpu.CompilerParams(
            dimension_semantics=("arbitrary", "arbitrary"), vmem_limit_bytes=VMEM_LIMIT),
        name="final",
    )(x1, mod, yt)


def _rope_tables(s):
    pos = jnp.arange(s, dtype=F32)
    inv_freq = ROPE_THETA ** (-jnp.arange(0, ROPE_DIM, 2, dtype=F32) / ROPE_DIM)
    ang = pos[:, None] * inv_freq[None, :]
    cos, sin = jnp.cos(ang), jnp.sin(ang)
    half = ROPE_DIM // 2
    ones = jnp.ones((s, HEAD_DIM - ROPE_DIM), F32)
    zeros = jnp.zeros((s, HEAD_DIM - ROPE_DIM), F32)
    zh = jnp.zeros((s, half), F32)
    rc = jnp.concatenate([cos, cos, ones], axis=1)
    rs1 = jnp.concatenate([-sin, zh, zeros], axis=1)
    rs2 = jnp.concatenate([zh, sin, zeros], axis=1)
    tile = lambda t: jnp.tile(t, (1, LANES // HEAD_DIM))
    return tile(rc), tile(rs1), tile(rs2)


def kernel(x, c, w_ada, b_ada, g_norm1, w_in, g_q, g_k, sinks, conv_w, g_out_attn,
           g_out_conv, w_out, g_norm2, w_pq, peer_subkeys, peer_u, peer_v):
    b, s, d = x.shape
    assert d == D_MODEL and s % WINDOW == 0 and (b * s) % LANES == 0

    mod = _ada(c, w_ada, b_ada).reshape(b, 6, d)

    rc, rs1, rs2 = _rope_tables(s)
    blk = jnp.arange(MXU_DIM) // HEAD_DIM
    bd = (blk[:, None] == blk[None, :]).astype(BF16)
    x1 = _mixer(x, mod, g_norm1.reshape(1, d), w_in.astype(BF16),
                jnp.tile(g_q, N_Q_HEADS).reshape(1, ATTN_WIDTH),
                jnp.tile(g_k, N_KV_HEADS).reshape(1, KV_WIDTH),
                rc, rs1, rs2, conv_w, g_out_attn.reshape(1, ATTN_WIDTH),
                g_out_conv.reshape(1, CONV_WIDTH), w_out.astype(BF16), bd, sinks)

    a_mat = _fold(peer_subkeys.astype(BF16), w_pq.astype(BF16))
    h2, rank2, e2w, cnt, e1w = _route(x1, mod, g_norm2.reshape(1, d), a_mat)
    ec = PEER_E1 * PEER_NKEYS
    vt = peer_v.astype(BF16).reshape(PEER_N // ec, ec, d).transpose(0, 2, 1)
    yt = _peer(h2, peer_u.astype(BF16), vt, rank2, e2w, cnt, e1w)
    return _final(x1, mod, yt)
```

```python
import functools
import math

import jax
import jax.numpy as jnp
from jax import lax
from jax.experimental import pallas as pl
from jax.experimental.pallas import tpu as pltpu

F32 = jnp.float32
BF16 = jnp.bfloat16

D_MODEL = 2048
HEAD_DIM = 64
N_Q_HEADS = 16
N_KV_HEADS = 4
Q_PER_KV = N_Q_HEADS // N_KV_HEADS
ATTN_WIDTH = N_Q_HEADS * HEAD_DIM
KV_WIDTH = N_KV_HEADS * HEAD_DIM
CONV_WIDTH = D_MODEL - ATTN_WIDTH
CONV_K = 3
IN_COLS = ATTN_WIDTH + 2 * KV_WIDTH + 3 * CONV_WIDTH
WINDOW = 128
ROPE_THETA = 500000.0
ROPE_DIM = HEAD_DIM // 4
ATTN_SCALE = HEAD_DIM ** -0.5
NEG_INF = -1e30
PEER_HEADS = 8
PEER_NKEYS = 128
PEER_N = PEER_NKEYS * PEER_NKEYS
PEER_HALF = 128
PEER_TOPK = 16
EPS = 1e-6

LANES = 128
MXU_DIM = 256
VMEM_LIMIT = 56 * 1024 * 1024

ADA_COLS = 1024
MIX_ROWS = 256
ROUTE_ROWS = 256
PEER_ROWS = 512
PEER_E1 = 8
E2_ROWS = 32
SCORE_ROWS = 512
VALUE_ROWS = 512

_PAIRS = tuple((a, b) for a in range(PEER_TOPK) for b in range(PEER_TOPK)
               if (a + 1) * (b + 1) <= PEER_TOPK)


def _dot_t(a, b):
    return lax.dot_general(a, b, (((1,), (1,)), ((), ())),
                           preferred_element_type=F32)


def _group_mean_sq(y, bd):
    y2 = y * y
    hi = y2.astype(BF16)
    lo = (y2 - hi.astype(F32)).astype(BF16)
    outs = []
    for j in range(y.shape[1] // MXU_DIM):
        sl = slice(MXU_DIM * j, MXU_DIM * (j + 1))
        outs.append(jnp.dot(hi[:, sl], bd, preferred_element_type=F32)
                    + jnp.dot(lo[:, sl], bd, preferred_element_type=F32))
    ss = outs[0] if len(outs) == 1 else jnp.concatenate(outs, axis=1)
    return ss * (1.0 / HEAD_DIM)


def _rope(t, rc, rs1, rs2):
    outs = []
    for j in range(t.shape[1] // LANES):
        tj = t[:, LANES * j:LANES * (j + 1)]
        outs.append(tj * rc
                    + pltpu.roll(tj, LANES - ROPE_DIM // 2, 1) * rs1
                    + pltpu.roll(tj, ROPE_DIM // 2, 1) * rs2)
    return outs[0] if len(outs) == 1 else jnp.concatenate(outs, axis=1)


def _ada_kernel(c_ref, w_ref, b_ref, o_ref):
    c = c_ref[...]
    s = c * jax.nn.sigmoid(c)
    o_ref[...] = jnp.dot(s, w_ref[...], preferred_element_type=F32,
                         precision=lax.Precision.HIGHEST) + b_ref[...]


def _ada(c, w_ada, b_ada):
    b, d = c.shape
    n = w_ada.shape[1]
    return pl.pallas_call(
        _ada_kernel,
        grid=(n // ADA_COLS,),
        in_specs=[pl.BlockSpec((b, d), lambda j: (0, 0)),
                  pl.BlockSpec((d, ADA_COLS), lambda j: (0, j)),
                  pl.BlockSpec((1, ADA_COLS), lambda j: (0, j))],
        out_specs=pl.BlockSpec((b, ADA_COLS), lambda j: (0, j)),
        out_shape=jax.ShapeDtypeStruct((b, n), F32),
        compiler_params=pltpu.CompilerParams(
            dimension_semantics=("arbitrary",), vmem_limit_bytes=VMEM_LIMIT),
        name="ada",
    )(c, w_ada, b_ada.reshape(1, n))


def _mixer_kernel(sinks_ref, x_ref, mod_ref, g1_ref, win_ref, gq_ref, gk_ref,
                  rc_ref, rs1_ref, rs2_ref, cw_ref, goa_ref, goc_ref, wout_ref,
                  bd_ref, o_ref, kk_ref, vv_ref, u_ref, qn_ref, mixed_ref):
    ts = x_ref.shape[0]
    si = pl.program_id(1)
    x = x_ref[...]
    mod = mod_ref[...]
    sh1, sc1, gt1 = mod[0:1], mod[1:2], mod[2:3]
    bd = bd_ref[...]
    rc, rs1, rs2 = rc_ref[...], rs1_ref[...], rs2_ref[...]

    ms = jnp.mean(x * x, axis=-1, keepdims=True)
    h1 = ((x * lax.rsqrt(ms + EPS)) * g1_ref[...]) * (1.0 + sc1) + sh1
    h1b = h1.astype(BF16)

    @pl.when(si == 0)
    def _():
        kk_ref[:, :, 0:WINDOW, :] = jnp.zeros((N_KV_HEADS, 2, WINDOW, LANES), BF16)
        vv_ref[:, :, 0:WINDOW, :] = jnp.zeros((N_KV_HEADS, 2, WINDOW, LANES), BF16)
        u_ref[0:8, :] = jnp.zeros((8, CONV_WIDTH), F32)

    @pl.when(si > 0)
    def _():
        kk_ref[:, :, 0:WINDOW, :] = kk_ref[:, :, ts:ts + WINDOW, :]
        vv_ref[:, :, 0:WINDOW, :] = vv_ref[:, :, ts:ts + WINDOW, :]
        u_ref[0:8, :] = u_ref[ts:ts + 8, :]

    def proj(lo, width):
        return jnp.dot(h1b, win_ref[:, lo:lo + width], preferred_element_type=F32)

    q = proj(0, ATTN_WIDTH)
    q = q * lax.rsqrt(_group_mean_sq(q, bd) + EPS) * gq_ref[...]
    qn_ref[...] = (_rope(q, rc, rs1, rs2) * ATTN_SCALE).astype(BF16)

    k = proj(ATTN_WIDTH, KV_WIDTH)
    k = k * lax.rsqrt(_group_mean_sq(k, bd) + EPS) * gk_ref[...]
    k = _rope(k, rc, rs1, rs2)
    v = proj(ATTN_WIDTH + KV_WIDTH, KV_WIDTH)
    low = lax.broadcasted_iota(jnp.int32, (1, LANES), 1) < HEAD_DIM
    for g in range(N_KV_HEADS):
        col = slice(LANES * (g // 2), LANES * (g // 2 + 1))
        for t, dst in ((k[:, col], kk_ref), (v[:, col], vv_ref)):
            tr = pltpu.roll(t, HEAD_DIM, 1)
            in_low, in_high = (t, tr) if g % 2 == 0 else (tr, t)
            dst[g, 0, WINDOW:WINDOW + ts, :] = jnp.where(low, in_low, 0.0).astype(BF16)
            dst[g, 1, WINDOW:WINDOW + ts, :] = jnp.where(low, 0.0, in_high).astype(BF16)

    qi = lax.broadcasted_iota(jnp.int32, (WINDOW, 2 * WINDOW), 0)
    kj = lax.broadcasted_iota(jnp.int32, (WINDOW, 2 * WINDOW), 1)
    band = (kj > qi) & (kj <= qi + WINDOW)

    def scores(jb, g):
        qrows = slice(jb * WINDOW, (jb + 1) * WINDOW)
        krows = slice(jb * WINDOW, (jb + 2) * WINDOW)
        qa = qn_ref[qrows, MXU_DIM * g:MXU_DIM * g + LANES]
        qb = qn_ref[qrows, MXU_DIM * g + LANES:MXU_DIM * (g + 1)]
        q2 = jnp.concatenate([qa, qb], axis=0)
        return (_dot_t(q2, kk_ref[g, 0, krows, :]),
                _dot_t(q2, kk_ref[g, 1, krows, :]))

    def attend(jb, g, s_lo, s_hi):
        first_key = jnp.where(si * (ts // WINDOW) + jb == 0, WINDOW, 0)
        mask = band & (kj >= first_key)
        qrows = slice(jb * WINDOW, (jb + 1) * WINDOW)
        krows = slice(jb * WINDOW, (jb + 2) * WINDOW)
        vcat = jnp.concatenate([vv_ref[g, 0, krows, :], vv_ref[g, 1, krows, :]], axis=0)
        halves = []
        for half in range(2):
            rows = slice(half * WINDOW, (half + 1) * WINDOW)
            ps, inv = [], []
            for o, s in enumerate((s_lo[rows], s_hi[rows])):
                sink = sinks_ref[Q_PER_KV * g + 2 * half + o]
                s = jnp.where(mask, s, NEG_INF)
                m = jnp.maximum(jnp.max(s, axis=-1, keepdims=True), sink)
                p = jnp.exp(s - m)
                denom = jnp.sum(p, axis=-1, keepdims=True) + jnp.exp(sink - m)
                ps.append(p.astype(BF16))
                inv.append(1.0 / denom)
            pv = jnp.dot(jnp.concatenate(ps, axis=1), vcat, preferred_element_type=F32)
            halves.append(pv * jnp.where(low, inv[0], inv[1]))
        att = jnp.concatenate(halves, axis=1)
        cols = slice(MXU_DIM * g, MXU_DIM * (g + 1))
        att = att * lax.rsqrt(_group_mean_sq(att, bd) + EPS) * goa_ref[:, cols]
        mixed_ref[qrows, cols] = att.astype(BF16)

    o0 = ATTN_WIDTH + 2 * KV_WIDTH
    cw = cw_ref[...]

    def conv_block(cb):
        cols = slice(cb * MXU_DIM, (cb + 1) * MXU_DIM)
        bg = proj(o0 + cb * MXU_DIM, MXU_DIM)
        u_ref[8:8 + ts, cols] = (proj(o0 + CONV_WIDTH + cb * MXU_DIM, MXU_DIM)
                                 * proj(o0 + 2 * CONV_WIDTH + cb * MXU_DIM, MXU_DIM))
        acc = cw[0:1, cols] * u_ref[6:6 + ts, cols]
        acc = acc + cw[1:2, cols] * u_ref[7:7 + ts, cols]
        acc = acc + cw[2:3, cols] * u_ref[8:8 + ts, cols]
        y = bg * acc
        y = y * lax.rsqrt(_group_mean_sq(y, bd) + EPS) * goc_ref[:, cols]
        mixed_ref[:, ATTN_WIDTH + cb * MXU_DIM:ATTN_WIDTH + (cb + 1) * MXU_DIM] = y.astype(BF16)

    iters = [(jb, g) for jb in range(ts // WINDOW) for g in range(N_KV_HEADS)]
    n_conv = CONV_WIDTH // MXU_DIM
    conv_done = 0
    s_cur = scores(*iters[0])
    for i, (jb, g) in enumerate(iters):
        s_next = scores(*iters[i + 1]) if i + 1 < len(iters) else None
        want = -(-n_conv * (i + 1) // len(iters))
        while conv_done < want:
            conv_block(conv_done)
            conv_done += 1
        attend(jb, g, *s_cur)
        s_cur = s_next

    o_ref[...] = x + gt1 * jnp.dot(mixed_ref[...], wout_ref[...], preferred_element_type=F32)


def _mixer(x, mod, g_norm1, w_in_b, gq, gk, rc, rs1, rs2, conv_w, goa, goc, w_out_b, bd, sinks):
    b, s, d = x.shape
    ts = min(MIX_ROWS, s)
    const = lambda shape: pl.BlockSpec(shape, lambda bi, si: (0,) * len(shape),
                                       pipeline_mode=pl.Buffered(1))
    rope_spec = pl.BlockSpec((ts, LANES), lambda bi, si: (si, 0))
    return pl.pallas_call(
        _mixer_kernel,
        grid=(b, s // ts),
        in_specs=[pl.BlockSpec(memory_space=pltpu.SMEM),
                  pl.BlockSpec((None, ts, d), lambda bi, si: (bi, si, 0)),
                  pl.BlockSpec((None, 6, d), lambda bi, si: (bi, 0, 0)),
                  const((1, d)),
                  const((d, IN_COLS)),
                  const((1, ATTN_WIDTH)),
                  const((1, KV_WIDTH)),
                  rope_spec, rope_spec, rope_spec,
                  const((CONV_K, CONV_WIDTH)),
                  const((1, ATTN_WIDTH)),
                  const((1, CONV_WIDTH)),
                  const((d, d)),
                  const((MXU_DIM, MXU_DIM))],
        out_specs=pl.BlockSpec((None, ts, d), lambda bi, si: (bi, si, 0)),
        out_shape=jax.ShapeDtypeStruct((b, s, d), F32),
        scratch_shapes=[pltpu.VMEM((N_KV_HEADS, 2, ts + WINDOW, LANES), BF16),
                        pltpu.VMEM((N_KV_HEADS, 2, ts + WINDOW, LANES), BF16),
                        pltpu.VMEM((ts + 8, CONV_WIDTH), F32),
                        pltpu.VMEM((ts, ATTN_WIDTH), BF16),
                        pltpu.VMEM((ts, d), BF16)],
        compiler_params=pltpu.CompilerParams(
            dimension_semantics=("arbitrary", "arbitrary"), vmem_limit_bytes=VMEM_LIMIT),
        name="mixer",
    )(sinks, x, mod, g_norm1, w_in_b, gq, gk, rc, rs1, rs2, conv_w, goa, goc, w_out_b, bd)


def _fold_kernel(sk_ref, w_ref, o_ref):
    o_ref[...] = _dot_t(sk_ref[...], w_ref[...]).astype(BF16)


def _fold(subkeys_b, w_pq_b):
    d = w_pq_b.shape[0]
    return pl.pallas_call(
        _fold_kernel,
        grid=(2, PEER_HEADS),
        in_specs=[pl.BlockSpec((None, None, PEER_NKEYS, PEER_HALF), lambda p, h: (h, p, 0, 0)),
                  pl.BlockSpec((d, PEER_HALF), lambda p, h: (0, 2 * h + p))],
        out_specs=pl.BlockSpec((PEER_NKEYS, d), lambda p, h: (p * PEER_HEADS + h, 0)),
        out_shape=jax.ShapeDtypeStruct((2 * PEER_HEADS * PEER_NKEYS, d), BF16),
        compiler_params=pltpu.CompilerParams(
            dimension_semantics=("arbitrary", "arbitrary"), vmem_limit_bytes=VMEM_LIMIT),
        name="fold",
    )(subkeys_b, w_pq_b)


def _top16(s, want_rank=False):
    vals = []
    rank = jnp.full(s.shape, float(PEER_TOPK), F32) if want_rank else None
    for it in range(PEER_TOPK):
        m = jnp.max(s, axis=0, keepdims=True)
        vals.append(m)
        hit = s == m
        if want_rank:
            rank = jnp.where(hit, float(it), rank)
        if it + 1 < PEER_TOPK:
            s = jnp.where(hit, -jnp.inf, s)
    return (vals, rank) if want_rank else vals


def _route_kernel(x_ref, mod_ref, g2_ref, a_ref, h2_ref, rank2_ref, e2w_ref, cnt_ref, e1w_ref, st_ref):
    x = x_ref[...]
    mod = mod_ref[...]
    sh2, sc2 = mod[3:4], mod[4:5]
    ms = jnp.mean(x * x, axis=-1, keepdims=True)
    h2 = ((x * lax.rsqrt(ms + EPS)) * g2_ref[...]) * (1.0 + sc2) + sh2
    h2b = h2.astype(BF16)
    h2_ref[...] = h2b
    st_ref[...] = _dot_t(a_ref[...], h2b)

    hk = PEER_HEADS * PEER_NKEYS
    for g in range(x.shape[0] // LANES):
        lanes = slice(g * LANES, (g + 1) * LANES)
        t1, t2 = [], []
        for h in range(PEER_HEADS):
            t1.append(_top16(st_ref[h * PEER_NKEYS:(h + 1) * PEER_NKEYS, lanes]))
            vals, rank = _top16(st_ref[hk + h * PEER_NKEYS:hk + (h + 1) * PEER_NKEYS, lanes], True)
            t2.append(vals)
            rank2_ref[g, h] = rank.astype(BF16)
        r1 = [jnp.concatenate([t1[h][a] for h in range(PEER_HEADS)], axis=0) for a in range(PEER_TOPK)]
        r2 = [jnp.concatenate([t2[h][a] for h in range(PEER_HEADS)], axis=0) for a in range(PEER_TOPK)]
        zs = [r1[a] + r2[b] for (a, b) in _PAIRS]
        cur = list(zs)
        z16 = None
        for it in range(PEER_TOPK):
            z16 = functools.reduce(jnp.maximum, cur)
            if it + 1 < PEER_TOPK:
                cur = [jnp.where(c == z16, -jnp.inf, c) for c in cur]
        zmax = zs[0]
        zsum = jnp.zeros_like(zmax)
        cnt = [jnp.zeros_like(zmax) for _ in range(PEER_TOPK)]
        for (a, b), z in zip(_PAIRS, zs):
            sel = z >= z16
            zsum = zsum + jnp.where(sel, jnp.exp(z - zmax), 0.0)
            cnt[a] = cnt[a] + jnp.where(sel, 1.0, 0.0)
        inv_z = 1.0 / zsum
        for h in range(PEER_HEADS):
            s1 = st_ref[h * PEER_NKEYS:(h + 1) * PEER_NKEYS, lanes]
            s2 = st_ref[hk + h * PEER_NKEYS:hk + (h + 1) * PEER_NKEYS, lanes]
            cnt_full = jnp.zeros_like(s1)
            for a in range(PEER_TOPK):
                cnt_full = jnp.where(s1 == t1[h][a], cnt[a][h:h + 1, :], cnt_full)
            e2w_ref[g, h] = (jnp.exp(s2 - t2[h][0]) * inv_z[h:h + 1, :]).astype(BF16)
            cnt_ref[g, h] = cnt_full
            e1w_ref[g, h] = jnp.exp(s1 - t1[h][0])


def _route(x1, mod, g_norm2, a_mat):
    b, s, d = x1.shape
    n = b * s
    tr = min(ROUTE_ROWS, s)
    per_seq = s // tr
    grp = tr // LANES
    rshape = (n // LANES, PEER_HEADS, PEER_NKEYS, LANES)
    rspec = pl.BlockSpec((grp, PEER_HEADS, PEER_NKEYS, LANES), lambda i: (i, 0, 0, 0))
    return pl.pallas_call(
        _route_kernel,
        grid=(n // tr,),
        in_specs=[pl.BlockSpec((None, tr, d), lambda i: (i // per_seq, i % per_seq, 0)),
                  pl.BlockSpec((None, 6, d), lambda i: (i // per_seq, 0, 0)),
                  pl.BlockSpec((1, d), lambda i: (0, 0), pipeline_mode=pl.Buffered(1)),
                  pl.BlockSpec(a_mat.shape, lambda i: (0, 0), pipeline_mode=pl.Buffered(1))],
        out_specs=[pl.BlockSpec((tr, d), lambda i: (i, 0)), rspec, rspec, rspec, rspec],
        out_shape=[jax.ShapeDtypeStruct((n, d), BF16),
                   jax.ShapeDtypeStruct(rshape, BF16), jax.ShapeDtypeStruct(rshape, BF16),
                   jax.ShapeDtypeStruct(rshape, F32), jax.ShapeDtypeStruct(rshape, F32)],
        scratch_shapes=[pltpu.VMEM((a_mat.shape[0], tr), F32)],
        compiler_params=pltpu.CompilerParams(
            dimension_semantics=("arbitrary",), vmem_limit_bytes=VMEM_LIMIT),
        name="route",
    )(x1, mod, g_norm2, a_mat)


def _peer_kernel(h2_ref, u_ref, vt_ref, rank2_ref, e2w_ref, cnt_ref, e1w_ref, x_ref, mod_ref, o_ref,
                 a0_scr, a1_scr, act0_scr, act1_scr, yt_ref):
    g = pl.program_id(0)
    groups, ec = a0_scr.shape[0], a0_scr.shape[1]
    d = yt_ref.shape[0]
    n_chunks = PEER_NKEYS // PEER_E1

    @pl.when(g == 0)
    def _():
        for ref in (a0_scr, a1_scr, act0_scr, act1_scr):
            ref[...] = jnp.zeros(ref.shape, ref.dtype)

    @pl.when((g == 0) | ((g >= 2) & ((g - 2) % n_chunks == 0)))
    def _():
        yt_ref[...] = jnp.zeros(yt_ref.shape, F32)

    sqrt_half = math.sqrt(0.5)

    def run(a_new, a_old, act_new, act_old):
        nblk = PEER_NKEYS // E2_ROWS

        def gate_thunks(e1l, tb):
            w = [jnp.zeros((E2_ROWS, LANES), BF16) for _ in range(nblk)]

            def head_part(h):
                cnt = jnp.broadcast_to(cnt_ref[tb, h, e1l:e1l + 1, :], (E2_ROWS, LANES)).astype(BF16)
                w1 = jnp.broadcast_to(e1w_ref[tb, h, e1l:e1l + 1, :], (E2_ROWS, LANES)).astype(BF16)
                for eb in range(nblk):
                    e2rows = slice(eb * E2_ROWS, (eb + 1) * E2_ROWS)
                    w[eb] = w[eb] + w1 * jnp.where(rank2_ref[tb, h, e2rows, :] < cnt,
                                                   e2w_ref[tb, h, e2rows, :], jnp.zeros((), BF16))

            def finish_part(eb):
                rows = slice(e1l * PEER_NKEYS + eb * E2_ROWS,
                             e1l * PEER_NKEYS + (eb + 1) * E2_ROWS)
                a = a_old[tb, rows, :]
                gelu = 0.5 * a * (1.0 + lax.erf(a * sqrt_half))
                lanes = slice((tb % 2) * LANES, (tb % 2 + 1) * LANES)
                act_new[tb // 2, rows, lanes] = gelu.astype(BF16) * w[eb]

            return ([functools.partial(head_part, h) for h in range(PEER_HEADS)]
                    + [functools.partial(finish_part, eb) for eb in range(nblk)])

        thunks = [t for e1l in range(PEER_E1) for tb in range(groups) for t in gate_thunks(e1l, tb)]

        halves = groups // 2
        kt_score, kt_value = d // MXU_DIM, ec // MXU_DIM
        score_rows, value_rows = min(SCORE_ROWS, ec), min(VALUE_ROWS, d)
        mxu_ops = [("score", rb, nb) for rb in range(ec // score_rows) for nb in range(halves)]
        mxu_ops += [("value", rb, nb) for rb in range(d // value_rows) for nb in range(halves)]
        n_small = halves * ((ec // MXU_DIM) * kt_score + (d // MXU_DIM) * kt_value)
        done_small, done_thunks = 0, 0

        def vpu_fill():
            nonlocal done_thunks
            target = -(-len(thunks) * done_small // n_small)
            while done_thunks < min(target, len(thunks)):
                thunks[done_thunks]()
                done_thunks += 1

        for kind, rb, nb in mxu_ops:
            tok = slice(nb * MXU_DIM, (nb + 1) * MXU_DIM)
            acc = None
            if kind == "score":
                rows = slice(rb * score_rows, (rb + 1) * score_rows)
                for k in range(kt_score):
                    ks = slice(k * MXU_DIM, (k + 1) * MXU_DIM)
                    part = _dot_t(u_ref[rows, ks], h2_ref[tok, ks])
                    acc = part if acc is None else acc + part
                    done_small += score_rows // MXU_DIM
                    vpu_fill()
                a_new[2 * nb, rows, :] = acc[:, :LANES]
                a_new[2 * nb + 1, rows, :] = acc[:, LANES:]
            else:
                rows = slice(rb * value_rows, (rb + 1) * value_rows)
                for k in range(kt_value):
                    ks = slice(k * MXU_DIM, (k + 1) * MXU_DIM)
                    part = jnp.dot(vt_ref[rows, ks], act_old[nb, ks, :], preferred_element_type=F32)
                    acc = part if acc is None else acc + part
                    done_small += value_rows // MXU_DIM
                    vpu_fill()
                yt_ref[rows, tok] += acc
        assert done_thunks == len(thunks)

    @pl.when(g % 2 == 0)
    def _():
        run(a0_scr, a1_scr, act1_scr, act0_scr)

    @pl.when(g % 2 == 1)
    def _():
        run(a1_scr, a0_scr, act0_scr, act1_scr)

    @pl.when((g >= 2) & ((g - 2) % n_chunks == n_chunks - 1))
    def _():
        gt2 = mod_ref[...][5:6]
        o_ref[...] = x_ref[...] + gt2 * yt_ref[...].T


def _peer(h2, u_b, vt_b, rank2, e2w, cnt, e1w, x1, mod):
    n, d = h2.shape
    seq = x1.shape[1]
    tp = min(PEER_ROWS, seq)
    grp = tp // LANES
    ec = PEER_E1 * PEER_NKEYS
    nc = PEER_N // ec
    total = (n // tp) * nc
    lag = lambda g, k: jnp.clip(g - k, 0, total - 1)
    full = pl.BlockSpec((grp, PEER_HEADS, PEER_NKEYS, LANES), lambda g: (lag(g, 1) // nc, 0, 0, 0))
    part = pl.BlockSpec((grp, PEER_HEADS, PEER_E1, LANES),
                        lambda g: (lag(g, 1) // nc, 0, lag(g, 1) % nc, 0))
    return pl.pallas_call(
        _peer_kernel,
        grid=(total + 2,),
        in_specs=[pl.BlockSpec((tp, d), lambda g: (lag(g, 0) // nc, 0)),
                  pl.BlockSpec((ec, d), lambda g: (lag(g, 0) % nc, 0)),
                  pl.BlockSpec((None, d, ec), lambda g: (lag(g, 2) % nc, 0, 0)),
                  full, full, part, part,
                  pl.BlockSpec((tp, d), lambda g: (lag(g, 2) // nc, 0)),
                  pl.BlockSpec((None, 6, d), lambda g: (lag(g, 2) // nc // (seq // tp), 0, 0))],
        out_specs=pl.BlockSpec((tp, d), lambda g: (lag(g, 2) // nc, 0)),
        out_shape=jax.ShapeDtypeStruct((n, d), F32),
        scratch_shapes=[pltpu.VMEM((grp, ec, LANES), F32), pltpu.VMEM((grp, ec, LANES), F32),
                        pltpu.VMEM((grp // 2, ec, MXU_DIM), BF16),
                        pltpu.VMEM((grp // 2, ec, MXU_DIM), BF16),
                        pltpu.VMEM((d, tp), F32)],
        compiler_params=pltpu.CompilerParams(
            dimension_semantics=("arbitrary",), vmem_limit_bytes=VMEM_LIMIT),
        name="peer",
    )(h2, u_b, vt_b, rank2, e2w, cnt, e1w, x1.reshape(n, d), mod)


def _rope_tables(s):
    pos = jnp.arange(s, dtype=F32)
    inv_freq = ROPE_THETA ** (-jnp.arange(0, ROPE_DIM, 2, dtype=F32) / ROPE_DIM)
    ang = pos[:, None] * inv_freq[None, :]
    cos, sin = jnp.cos(ang), jnp.sin(ang)
    half = ROPE_DIM // 2
    ones = jnp.ones((s, HEAD_DIM - ROPE_DIM), F32)
    zeros = jnp.zeros((s, HEAD_DIM - ROPE_DIM), F32)
    zh = jnp.zeros((s, half), F32)
    rc = jnp.concatenate([cos, cos, ones], axis=1)
    rs1 = jnp.concatenate([-sin, zh, zeros], axis=1)
    rs2 = jnp.concatenate([zh, sin, zeros], axis=1)
    tile = lambda t: jnp.tile(t, (1, LANES // HEAD_DIM))
    return tile(rc), tile(rs1), tile(rs2)


def kernel(x, c, w_ada, b_ada, g_norm1, w_in, g_q, g_k, sinks, conv_w, g_out_attn,
           g_out_conv, w_out, g_norm2, w_pq, peer_subkeys, peer_u, peer_v):
    b, s, d = x.shape
    assert d == D_MODEL and s % WINDOW == 0 and (b * s) % LANES == 0

    mod = _ada(c, w_ada, b_ada).reshape(b, 6, d)

    rc, rs1, rs2 = _rope_tables(s)
    blk = jnp.arange(MXU_DIM) // HEAD_DIM
    bd = (blk[:, None] == blk[None, :]).astype(BF16)
    x1 = _mixer(x, mod, g_norm1.reshape(1, d), w_in.astype(BF16),
                jnp.tile(g_q, N_Q_HEADS).reshape(1, ATTN_WIDTH),
                jnp.tile(g_k, N_KV_HEADS).reshape(1, KV_WIDTH),
                rc, rs1, rs2, conv_w, g_out_attn.reshape(1, ATTN_WIDTH),
                g_out_conv.reshape(1, CONV_WIDTH), w_out.astype(BF16), bd, sinks)

    a_mat = _fold(peer_subkeys.astype(BF16), w_pq.astype(BF16))
    h2, rank2, e2w, cnt, e1w = _route(x1, mod, g_norm2.reshape(1, d), a_mat)
    ec = PEER_E1 * PEER_NKEYS
    vt = peer_v.astype(BF16).reshape(PEER_N // ec, ec, d).transpose(0, 2, 1)
    out = _peer(h2, peer_u.astype(BF16), vt, rank2, e2w, cnt, e1w, x1, mod)
    return out.reshape(b, s, d)
```

```python
import functools
import math

import jax
import jax.numpy as jnp
from jax import lax
from jax.experimental import pallas as pl
from jax.experimental.pallas import tpu as pltpu

F32 = jnp.float32
BF16 = jnp.bfloat16

D_MODEL = 2048
HEAD_DIM = 64
N_Q_HEADS = 16
N_KV_HEADS = 4
Q_PER_KV = N_Q_HEADS // N_KV_HEADS
ATTN_WIDTH = N_Q_HEADS * HEAD_DIM
KV_WIDTH = N_KV_HEADS * HEAD_DIM
CONV_WIDTH = D_MODEL - ATTN_WIDTH
CONV_K = 3
IN_COLS = ATTN_WIDTH + 2 * KV_WIDTH + 3 * CONV_WIDTH
WINDOW = 128
ROPE_THETA = 500000.0
ROPE_DIM = HEAD_DIM // 4
ATTN_SCALE = HEAD_DIM ** -0.5
NEG_INF = -1e30
PEER_HEADS = 8
PEER_NKEYS = 128
PEER_N = PEER_NKEYS * PEER_NKEYS
PEER_HALF = 128
PEER_TOPK = 16
EPS = 1e-6

LANES = 128
MXU_DIM = 256
VMEM_LIMIT = 56 * 1024 * 1024

ADA_COLS = 1024
MIX_ROWS = 256
ROUTE_ROWS = 256
PEER_ROWS = 512
PEER_E1 = 8
E2_ROWS = 32
SCORE_ROWS = 512
VALUE_ROWS = 512

_PAIRS = tuple((a, b) for a in range(PEER_TOPK) for b in range(PEER_TOPK)
               if (a + 1) * (b + 1) <= PEER_TOPK)


def _dot_t(a, b):
    return lax.dot_general(a, b, (((1,), (1,)), ((), ())),
                           preferred_element_type=F32)


def _group_mean_sq(y, bd):
    y2 = y * y
    hi = y2.astype(BF16)
    lo = (y2 - hi.astype(F32)).astype(BF16)
    outs = []
    for j in range(y.shape[1] // MXU_DIM):
        sl = slice(MXU_DIM * j, MXU_DIM * (j + 1))
        outs.append(jnp.dot(hi[:, sl], bd, preferred_element_type=F32)
                    + jnp.dot(lo[:, sl], bd, preferred_element_type=F32))
    ss = outs[0] if len(outs) == 1 else jnp.concatenate(outs, axis=1)
    return ss * (1.0 / HEAD_DIM)


def _rope(t, rc, rs1, rs2):
    outs = []
    for j in range(t.shape[1] // LANES):
        tj = t[:, LANES * j:LANES * (j + 1)]
        outs.append(tj * rc
                    + pltpu.roll(tj, LANES - ROPE_DIM // 2, 1) * rs1
                    + pltpu.roll(tj, ROPE_DIM // 2, 1) * rs2)
    return outs[0] if len(outs) == 1 else jnp.concatenate(outs, axis=1)


def _ada_kernel(c_ref, w_ref, b_ref, o_ref):
    c = c_ref[...]
    s = c * jax.nn.sigmoid(c)
    o_ref[...] = jnp.dot(s, w_ref[...], preferred_element_type=F32,
                         precision=lax.Precision.HIGHEST) + b_ref[...]


def _ada(c, w_ada, b_ada):
    b, d = c.shape
    n = w_ada.shape[1]
    return pl.pallas_call(
        _ada_kernel,
        grid=(n // ADA_COLS,),
        in_specs=[pl.BlockSpec((b, d), lambda j: (0, 0)),
                  pl.BlockSpec((d, ADA_COLS), lambda j: (0, j)),
                  pl.BlockSpec((1, ADA_COLS), lambda j: (0, j))],
        out_specs=pl.BlockSpec((b, ADA_COLS), lambda j: (0, j)),
        out_shape=jax.ShapeDtypeStruct((b, n), F32),
        compiler_params=pltpu.CompilerParams(
            dimension_semantics=("arbitrary",), vmem_limit_bytes=VMEM_LIMIT),
        name="ada",
    )(c, w_ada, b_ada.reshape(1, n))


def _mixer_kernel(sinks_ref, x_ref, mod_ref, g1_ref, win_ref, gq_ref, gk_ref,
                  rc_ref, rs1_ref, rs2_ref, cw_ref, goa_ref, goc_ref, wout_ref,
                  bd_ref, o_ref, kk_ref, vv_ref, u_ref, qn_ref, mixed_ref):
    ts = x_ref.shape[0]
    si = pl.program_id(1)
    x = x_ref[...]
    mod = mod_ref[...]
    sh1, sc1, gt1 = mod[0:1], mod[1:2], mod[2:3]
    bd = bd_ref[...]
    rc, rs1, rs2 = rc_ref[...], rs1_ref[...], rs2_ref[...]

    ms = jnp.mean(x * x, axis=-1, keepdims=True)
    h1 = ((x * lax.rsqrt(ms + EPS)) * g1_ref[...]) * (1.0 + sc1) + sh1
    h1b = h1.astype(BF16)

    @pl.when(si == 0)
    def _():
        kk_ref[:, :, 0:WINDOW, :] = jnp.zeros((N_KV_HEADS, 2, WINDOW, LANES), BF16)
        vv_ref[:, :, 0:WINDOW, :] = jnp.zeros((N_KV_HEADS, 2, WINDOW, LANES), BF16)
        u_ref[0:8, :] = jnp.zeros((8, CONV_WIDTH), F32)

    @pl.when(si > 0)
    def _():
        kk_ref[:, :, 0:WINDOW, :] = kk_ref[:, :, ts:ts + WINDOW, :]
        vv_ref[:, :, 0:WINDOW, :] = vv_ref[:, :, ts:ts + WINDOW, :]
        u_ref[0:8, :] = u_ref[ts:ts + 8, :]

    def proj(lo, width):
        return jnp.dot(h1b, win_ref[:, lo:lo + width], preferred_element_type=F32)

    q = proj(0, ATTN_WIDTH)
    q = q * lax.rsqrt(_group_mean_sq(q, bd) + EPS) * gq_ref[...]
    qn_ref[...] = (_rope(q, rc, rs1, rs2) * ATTN_SCALE).astype(BF16)

    k = proj(ATTN_WIDTH, KV_WIDTH)
    k = k * lax.rsqrt(_group_mean_sq(k, bd) + EPS) * gk_ref[...]
    k = _rope(k, rc, rs1, rs2)
    v = proj(ATTN_WIDTH + KV_WIDTH, KV_WIDTH)
    low = lax.broadcasted_iota(jnp.int32, (1, LANES), 1) < HEAD_DIM
    for g in range(N_KV_HEADS):
        col = slice(LANES * (g // 2), LANES * (g // 2 + 1))
        for t, dst in ((k[:, col], kk_ref), (v[:, col], vv_ref)):
            tr = pltpu.roll(t, HEAD_DIM, 1)
            in_low, in_high = (t, tr) if g % 2 == 0 else (tr, t)
            dst[g, 0, WINDOW:WINDOW + ts, :] = jnp.where(low, in_low, 0.0).astype(BF16)
            dst[g, 1, WINDOW:WINDOW + ts, :] = jnp.where(low, 0.0, in_high).astype(BF16)

    qi = lax.broadcasted_iota(jnp.int32, (WINDOW, 2 * WINDOW), 0)
    kj = lax.broadcasted_iota(jnp.int32, (WINDOW, 2 * WINDOW), 1)
    band = (kj > qi) & (kj <= qi + WINDOW)

    def scores(jb, g):
        qrows = slice(jb * WINDOW, (jb + 1) * WINDOW)
        krows = slice(jb * WINDOW, (jb + 2) * WINDOW)
        qa = qn_ref[qrows, MXU_DIM * g:MXU_DIM * g + LANES]
        qb = qn_ref[qrows, MXU_DIM * g + LANES:MXU_DIM * (g + 1)]
        q2 = jnp.concatenate([qa, qb], axis=0)
        return (_dot_t(q2, kk_ref[g, 0, krows, :]),
                _dot_t(q2, kk_ref[g, 1, krows, :]))

    def attend(jb, g, s_lo, s_hi):
        first_key = jnp.where(si * (ts // WINDOW) + jb == 0, WINDOW, 0)
        mask = band & (kj >= first_key)
        qrows = slice(jb * WINDOW, (jb + 1) * WINDOW)
        krows = slice(jb * WINDOW, (jb + 2) * WINDOW)
        vcat = jnp.concatenate([vv_ref[g, 0, krows, :], vv_ref[g, 1, krows, :]], axis=0)
        halves = []
        for half in range(2):
            rows = slice(half * WINDOW, (half + 1) * WINDOW)
            ps, inv = [], []
            for o, s in enumerate((s_lo[rows], s_hi[rows])):
                sink = sinks_ref[Q_PER_KV * g + 2 * half + o]
                s = jnp.where(mask, s, NEG_INF)
                m = jnp.maximum(jnp.max(s, axis=-1, keepdims=True), sink)
                p = jnp.exp(s - m)
                denom = jnp.sum(p, axis=-1, keepdims=True) + jnp.exp(sink - m)
                ps.append(p.astype(BF16))
                inv.append(1.0 / denom)
            pv = jnp.dot(jnp.concatenate(ps, axis=1), vcat, preferred_element_type=F32)
            halves.append(pv * jnp.where(low, inv[0], inv[1]))
        att = jnp.concatenate(halves, axis=1)
        cols = slice(MXU_DIM * g, MXU_DIM * (g + 1))
        att = att * lax.rsqrt(_group_mean_sq(att, bd) + EPS) * goa_ref[:, cols]
        mixed_ref[qrows, cols] = att.astype(BF16)

    o0 = ATTN_WIDTH + 2 * KV_WIDTH
    cw = cw_ref[...]

    def conv_block(cb):
        cols = slice(cb * MXU_DIM, (cb + 1) * MXU_DIM)
        bg = proj(o0 + cb * MXU_DIM, MXU_DIM)
        u_ref[8:8 + ts, cols] = (proj(o0 + CONV_WIDTH + cb * MXU_DIM, MXU_DIM)
                                 * proj(o0 + 2 * CONV_WIDTH + cb * MXU_DIM, MXU_DIM))
        acc = cw[0:1, cols] * u_ref[6:6 + ts, cols]
        acc = acc + cw[1:2, cols] * u_ref[7:7 + ts, cols]
        acc = acc + cw[2:3, cols] * u_ref[8:8 + ts, cols]
        y = bg * acc
        y = y * lax.rsqrt(_group_mean_sq(y, bd) + EPS) * goc_ref[:, cols]
        mixed_ref[:, ATTN_WIDTH + cb * MXU_DIM:ATTN_WIDTH + (cb + 1) * MXU_DIM] = y.astype(BF16)

    iters = [(jb, g) for jb in range(ts // WINDOW) for g in range(N_KV_HEADS)]
    n_conv = CONV_WIDTH // MXU_DIM
    conv_done = 0
    s_cur = scores(*iters[0])
    for i, (jb, g) in enumerate(iters):
        s_next = scores(*iters[i + 1]) if i + 1 < len(iters) else None
        want = -(-n_conv * (i + 1) // len(iters))
        while conv_done < want:
            conv_block(conv_done)
            conv_done += 1
        attend(jb, g, *s_cur)
        s_cur = s_next

    o_ref[...] = x + gt1 * jnp.dot(mixed_ref[...], wout_ref[...], preferred_element_type=F32)


def _mixer(x, mod, g_norm1, w_in_b, gq, gk, rc, rs1, rs2, conv_w, goa, goc, w_out_b, bd, sinks):
    b, s, d = x.shape
    ts = min(MIX_ROWS, s)
    const = lambda shape: pl.BlockSpec(shape, lambda bi, si: (0,) * len(shape),
                                       pipeline_mode=pl.Buffered(1))
    rope_spec = pl.BlockSpec((ts, LANES), lambda bi, si: (si, 0))
    return pl.pallas_call(
        _mixer_kernel,
        grid=(b, s // ts),
        in_specs=[pl.BlockSpec(memory_space=pltpu.SMEM),
                  pl.BlockSpec((None, ts, d), lambda bi, si: (bi, si, 0)),
                  pl.BlockSpec((None, 6, d), lambda bi, si: (bi, 0, 0)),
                  const((1, d)),
                  const((d, IN_COLS)),
                  const((1, ATTN_WIDTH)),
                  const((1, KV_WIDTH)),
                  rope_spec, rope_spec, rope_spec,
                  const((CONV_K, CONV_WIDTH)),
                  const((1, ATTN_WIDTH)),
                  const((1, CONV_WIDTH)),
                  const((d, d)),
                  const((MXU_DIM, MXU_DIM))],
        out_specs=pl.BlockSpec((None, ts, d), lambda bi, si: (bi, si, 0)),
        out_shape=jax.ShapeDtypeStruct((b, s, d), F32),
        scratch_shapes=[pltpu.VMEM((N_KV_HEADS, 2, ts + WINDOW, LANES), BF16),
                        pltpu.VMEM((N_KV_HEADS, 2, ts + WINDOW, LANES), BF16),
                        pltpu.VMEM((ts + 8, CONV_WIDTH), F32),
                        pltpu.VMEM((ts, ATTN_WIDTH), BF16),
                        pltpu.VMEM((ts, d), BF16)],
        compiler_params=pltpu.CompilerParams(
            dimension_semantics=("arbitrary", "arbitrary"), vmem_limit_bytes=VMEM_LIMIT),
        name="mixer",
    )(sinks, x, mod, g_norm1, w_in_b, gq, gk, rc, rs1, rs2, conv_w, goa, goc, w_out_b, bd)


def _fold_kernel(sk_ref, w_ref, o_ref):
    o_ref[...] = _dot_t(sk_ref[...], w_ref[...]).astype(BF16)


def _fold(subkeys_b, w_pq_b):
    d = w_pq_b.shape[0]
    return pl.pallas_call(
        _fold_kernel,
        grid=(2, PEER_HEADS),
        in_specs=[pl.BlockSpec((None, None, PEER_NKEYS, PEER_HALF), lambda p, h: (h, p, 0, 0)),
                  pl.BlockSpec((d, PEER_HALF), lambda p, h: (0, 2 * h + p))],
        out_specs=pl.BlockSpec((PEER_NKEYS, d), lambda p, h: (p * PEER_HEADS + h, 0)),
        out_shape=jax.ShapeDtypeStruct((2 * PEER_HEADS * PEER_NKEYS, d), BF16),
        compiler_params=pltpu.CompilerParams(
            dimension_semantics=("arbitrary", "arbitrary"), vmem_limit_bytes=VMEM_LIMIT),
        name="fold",
    )(subkeys_b, w_pq_b)


def _compare_exchange(v, i, l, descending):
    hi, lo = jnp.maximum(v[i], v[l]), jnp.minimum(v[i], v[l])
    v[i], v[l] = (hi, lo) if descending else (lo, hi)


def _bitonic_merge(v):
    n = len(v)
    j = n // 2
    while j >= 1:
        for i in range(n):
            if i ^ j > i:
                _compare_exchange(v, i, i ^ j, True)
        j //= 2
    return v


def _top16(s, want_rank=False):
    n = PEER_TOPK
    sub = s.shape[0] // n
    v = [s[sub * j:sub * (j + 1), :] for j in range(n)]
    k = 2
    while k <= n:
        j = k // 2
        while j >= 1:
            for i in range(n):
                if i ^ j > i:
                    _compare_exchange(v, i, i ^ j, (i & k) == 0)
            j //= 2
        k *= 2
    shift = sub // 2
    while shift >= 1:
        rolled = [pltpu.roll(w, shift, 0) for w in v]
        v = _bitonic_merge([jnp.maximum(v[i], rolled[n - 1 - i]) for i in range(n)])
        shift //= 2
    t = [w[0:1, :] for w in v]
    if not want_rank:
        return t
    b8 = t[7] > s
    b4 = jnp.where(b8, t[11], t[3]) > s
    b2 = jnp.where(b8, jnp.where(b4, t[13], t[9]), jnp.where(b4, t[5], t[1])) > s
    b1 = jnp.where(b8,
                   jnp.where(b4, jnp.where(b2, t[14], t[12]), jnp.where(b2, t[10], t[8])),
                   jnp.where(b4, jnp.where(b2, t[6], t[4]), jnp.where(b2, t[2], t[0]))) > s
    rank = (jnp.where(b8, 8.0, 0.0) + jnp.where(b4, 4.0, 0.0) + jnp.where(b2, 2.0, 0.0)
            + jnp.where(b1, 1.0, 0.0) + jnp.where(t[15] > s, 1.0, 0.0))
    return t, rank


def _route_kernel(x_ref, mod_ref, g2_ref, a_ref, h2_ref, rank2_ref, e2w_ref, cnt_ref, e1w_ref, st_ref):
    x = x_ref[...]
    mod = mod_ref[...]
    sh2, sc2 = mod[3:4], mod[4:5]
    ms = jnp.mean(x * x, axis=-1, keepdims=True)
    h2 = ((x * lax.rsqrt(ms + EPS)) * g2_ref[...]) * (1.0 + sc2) + sh2
    h2b = h2.astype(BF16)
    h2_ref[...] = h2b
    st_ref[...] = _dot_t(a_ref[...], h2b)

    hk = PEER_HEADS * PEER_NKEYS
    for g in range(x.shape[0] // LANES):
        lanes = slice(g * LANES, (g + 1) * LANES)
        t1, t2 = [], []
        for h in range(PEER_HEADS):
            t1.append(_top16(st_ref[h * PEER_NKEYS:(h + 1) * PEER_NKEYS, lanes]))
            vals, rank = _top16(st_ref[hk + h * PEER_NKEYS:hk + (h + 1) * PEER_NKEYS, lanes], True)
            t2.append(vals)
            rank2_ref[g, h] = rank.astype(BF16)
        r1 = [jnp.concatenate([t1[h][a] for h in range(PEER_HEADS)], axis=0) for a in range(PEER_TOPK)]
        r2 = [jnp.concatenate([t2[h][a] for h in range(PEER_HEADS)], axis=0) for a in range(PEER_TOPK)]
        zs = [r1[a] + r2[b] for (a, b) in _PAIRS]
        cur = list(zs)
        z16 = None
        for it in range(PEER_TOPK):
            z16 = functools.reduce(jnp.maximum, cur)
            if it + 1 < PEER_TOPK:
                cur = [jnp.where(c == z16, -jnp.inf, c) for c in cur]
        zmax = zs[0]
        zsum = jnp.zeros_like(zmax)
        cnt = [jnp.zeros_like(zmax) for _ in range(PEER_TOPK)]
        for (a, b), z in zip(_PAIRS, zs):
            sel = z >= z16
            zsum = zsum + jnp.where(sel, jnp.exp(z - zmax), 0.0)
            cnt[a] = cnt[a] + jnp.where(sel, 1.0, 0.0)
        inv_z = 1.0 / zsum
        for h in range(PEER_HEADS):
            s1 = st_ref[h * PEER_NKEYS:(h + 1) * PEER_NKEYS, lanes]
            s2 = st_ref[hk + h * PEER_NKEYS:hk + (h + 1) * PEER_NKEYS, lanes]
            cnt_full = jnp.zeros_like(s1)
            for a in range(PEER_TOPK):
                cnt_full = jnp.where(s1 == t1[h][a], cnt[a][h:h + 1, :], cnt_full)
            e2w_ref[g, h] = (jnp.exp(s2 - t2[h][0]) * inv_z[h:h + 1, :]).astype(BF16)
            cnt_ref[g, h] = cnt_full
            e1w_ref[g, h] = jnp.exp(s1 - t1[h][0])


def _route(x1, mod, g_norm2, a_mat):
    b, s, d = x1.shape
    n = b * s
    tr = min(ROUTE_ROWS, s)
    per_seq = s // tr
    grp = tr // LANES
    rshape = (n // LANES, PEER_HEADS, PEER_NKEYS, LANES)
    rspec = pl.BlockSpec((grp, PEER_HEADS, PEER_NKEYS, LANES), lambda i: (i, 0, 0, 0))
    return pl.pallas_call(
        _route_kernel,
        grid=(n // tr,),
        in_specs=[pl.BlockSpec((None, tr, d), lambda i: (i // per_seq, i % per_seq, 0)),
                  pl.BlockSpec((None, 6, d), lambda i: (i // per_seq, 0, 0)),
                  pl.BlockSpec((1, d), lambda i: (0, 0), pipeline_mode=pl.Buffered(1)),
                  pl.BlockSpec(a_mat.shape, lambda i: (0, 0), pipeline_mode=pl.Buffered(1))],
        out_specs=[pl.BlockSpec((tr, d), lambda i: (i, 0)), rspec, rspec, rspec, rspec],
        out_shape=[jax.ShapeDtypeStruct((n, d), BF16),
                   jax.ShapeDtypeStruct(rshape, BF16), jax.ShapeDtypeStruct(rshape, BF16),
                   jax.ShapeDtypeStruct(rshape, F32), jax.ShapeDtypeStruct(rshape, F32)],
        scratch_shapes=[pltpu.VMEM((a_mat.shape[0], tr), F32)],
        compiler_params=pltpu.CompilerParams(
            dimension_semantics=("arbitrary",), vmem_limit_bytes=VMEM_LIMIT),
        name="route",
    )(x1, mod, g_norm2, a_mat)


def _peer_kernel(h2_ref, u_ref, vt_ref, rank2_ref, e2w_ref, cnt_ref, e1w_ref, yt_ref,
                 a0_scr, a1_scr, act0_scr, act1_scr):
    g = pl.program_id(0)
    groups, ec = a0_scr.shape[0], a0_scr.shape[1]
    d = yt_ref.shape[0]
    n_chunks = PEER_NKEYS // PEER_E1

    @pl.when(g == 0)
    def _():
        for ref in (a0_scr, a1_scr, act0_scr, act1_scr):
            ref[...] = jnp.zeros(ref.shape, ref.dtype)

    @pl.when((g == 0) | ((g >= 2) & ((g - 2) % n_chunks == 0)))
    def _():
        yt_ref[...] = jnp.zeros(yt_ref.shape, F32)

    sqrt_half = math.sqrt(0.5)

    def run(a_new, a_old, act_new, act_old):
        nblk = PEER_NKEYS // E2_ROWS

        def gate_thunks(e1l, tb):
            w = [jnp.zeros((E2_ROWS, LANES), BF16) for _ in range(nblk)]

            def head_part(h):
                cnt = jnp.broadcast_to(cnt_ref[tb, h, e1l:e1l + 1, :], (E2_ROWS, LANES)).astype(BF16)
                w1 = jnp.broadcast_to(e1w_ref[tb, h, e1l:e1l + 1, :], (E2_ROWS, LANES)).astype(BF16)
                for eb in range(nblk):
                    e2rows = slice(eb * E2_ROWS, (eb + 1) * E2_ROWS)
                    w[eb] = w[eb] + w1 * jnp.where(rank2_ref[tb, h, e2rows, :] < cnt,
                                                   e2w_ref[tb, h, e2rows, :], jnp.zeros((), BF16))

            def finish_part(eb):
                rows = slice(e1l * PEER_NKEYS + eb * E2_ROWS,
                             e1l * PEER_NKEYS + (eb + 1) * E2_ROWS)
                a = a_old[tb, rows, :]
                gelu = 0.5 * a * (1.0 + lax.erf(a * sqrt_half))
                lanes = slice((tb % 2) * LANES, (tb % 2 + 1) * LANES)
                act_new[tb // 2, rows, lanes] = gelu.astype(BF16) * w[eb]

            return ([functools.partial(head_part, h) for h in range(PEER_HEADS)]
                    + [functools.partial(finish_part, eb) for eb in range(nblk)])

        thunks = [t for e1l in range(PEER_E1) for tb in range(groups) for t in gate_thunks(e1l, tb)]

        halves = groups // 2
        kt_score, kt_value = d // MXU_DIM, ec // MXU_DIM
        score_rows, value_rows = min(SCORE_ROWS, ec), min(VALUE_ROWS, d)
        mxu_ops = [("score", rb, nb) for rb in range(ec // score_rows) for nb in range(halves)]
        mxu_ops += [("value", rb, nb) for rb in range(d // value_rows) for nb in range(halves)]
        n_small = halves * ((ec // MXU_DIM) * kt_score + (d // MXU_DIM) * kt_value)
        done_small, done_thunks = 0, 0

        def vpu_fill():
            nonlocal done_thunks
            target = -(-len(thunks) * done_small // n_small)
            while done_thunks < min(target, len(thunks)):
                thunks[done_thunks]()
                done_thunks += 1

        for kind, rb, nb in mxu_ops:
            tok = slice(nb * MXU_DIM, (nb + 1) * MXU_DIM)
            acc = None
            if kind == "score":
                rows = slice(rb * score_rows, (rb + 1) * score_rows)
                for k in range(kt_score):
                    ks = slice(k * MXU_DIM, (k + 1) * MXU_DIM)
                    part = _dot_t(u_ref[rows, ks], h2_ref[tok, ks])
                    acc = part if acc is None else acc + part
                    done_small += score_rows // MXU_DIM
                    vpu_fill()
                a_new[2 * nb, rows, :] = acc[:, :LANES]
                a_new[2 * nb + 1, rows, :] = acc[:, LANES:]
            else:
                rows = slice(rb * value_rows, (rb + 1) * value_rows)
                for k in range(kt_value):
                    ks = slice(k * MXU_DIM, (k + 1) * MXU_DIM)
                    part = jnp.dot(vt_ref[rows, ks], act_old[nb, ks, :], preferred_element_type=F32)
                    acc = part if acc is None else acc + part
                    done_small += value_rows // MXU_DIM
                    vpu_fill()
                yt_ref[rows, tok] += acc
        assert done_thunks == len(thunks)

    @pl.when(g % 2 == 0)
    def _():
        run(a0_scr, a1_scr, act1_scr, act0_scr)

    @pl.when(g % 2 == 1)
    def _():
        run(a1_scr, a0_scr, act0_scr, act1_scr)


def _peer(h2, u_b, vt_b, rank2, e2w, cnt, e1w):
    n, d = h2.shape
    tp = min(PEER_ROWS, n)
    grp = tp // LANES
    ec = PEER_E1 * PEER_NKEYS
    nc = PEER_N // ec
    total = (n // tp) * nc
    lag = lambda g, k: jnp.clip(g - k, 0, total - 1)
    full = pl.BlockSpec((grp, PEER_HEADS, PEER_NKEYS, LANES), lambda g: (lag(g, 1) // nc, 0, 0, 0))
    part = pl.BlockSpec((grp, PEER_HEADS, PEER_E1, LANES),
                        lambda g: (lag(g, 1) // nc, 0, lag(g, 1) % nc, 0))
    return pl.pallas_call(
        _peer_kernel,
        grid=(total + 2,),
        in_specs=[pl.BlockSpec((tp, d), lambda g: (lag(g, 0) // nc, 0)),
                  pl.BlockSpec((ec, d), lambda g: (lag(g, 0) % nc, 0)),
                  pl.BlockSpec((None, d, ec), lambda g: (lag(g, 2) % nc, 0, 0)),
                  full, full, part, part],
        out_specs=pl.BlockSpec((d, tp), lambda g: (0, lag(g, 2) // nc)),
        out_shape=jax.ShapeDtypeStruct((d, n), F32),
        scratch_shapes=[pltpu.VMEM((grp, ec, LANES), F32), pltpu.VMEM((grp, ec, LANES), F32),
                        pltpu.VMEM((grp // 2, ec, MXU_DIM), BF16),
                        pltpu.VMEM((grp // 2, ec, MXU_DIM), BF16)],
        compiler_params=pltpu.CompilerParams(
            dimension_semantics=("arbitrary",), vmem_limit_bytes=VMEM_LIMIT),
        name="peer",
    )(h2, u_b, vt_b, rank2, e2w, cnt, e1w)


def _final_kernel(x_ref, mod_ref, yt_ref, o_ref):
    gt2 = mod_ref[...][5:6]
    o_ref[...] = x_ref[...] + gt2 * yt_ref[...].T


def _final(x1, mod, yt):
    b, s, d = x1.shape
    tf = min(PEER_ROWS, s)
    per_seq = s // tf
    return pl.pallas_call(
        _final_kernel,
        grid=(b, per_seq),
        in_specs=[pl.BlockSpec((None, tf, d), lambda bi, si: (bi, si, 0)),
                  pl.BlockSpec((None, 6, d), lambda bi, si: (bi, 0, 0)),
                  pl.BlockSpec((d, tf), lambda bi, si: (0, bi * per_seq + si))],
        out_specs=pl.BlockSpec((None, tf, d), lambda bi, si: (bi, si, 0)),
        out_shape=jax.ShapeDtypeStruct((b, s, d), F32),
        compiler_params=pltpu.CompilerParams(
            dimension_semantics=("arbitrary", "arbitrary"), vmem_limit_bytes=VMEM_LIMIT),
        name="final",
    )(x1, mod, yt)


def _rope_tables(s):
    pos = jnp.arange(s, dtype=F32)
    inv_freq = ROPE_THETA ** (-jnp.arange(0, ROPE_DIM, 2, dtype=F32) / ROPE_DIM)
    ang = pos[:, None] * inv_freq[None, :]
    cos, sin = jnp.cos(ang), jnp.sin(ang)
    half = ROPE_DIM // 2
    ones = jnp.ones((s, HEAD_DIM - ROPE_DIM), F32)
    zeros = jnp.zeros((s, HEAD_DIM - ROPE_DIM), F32)
    zh = jnp.zeros((s, half), F32)
    rc = jnp.concatenate([cos, cos, ones], axis=1)
    rs1 = jnp.concatenate([-sin, zh, zeros], axis=1)
    rs2 = jnp.concatenate([zh, sin, zeros], axis=1)
    tile = lambda t: jnp.tile(t, (1, LANES // HEAD_DIM))
    return tile(rc), tile(rs1), tile(rs2)


def kernel(x, c, w_ada, b_ada, g_norm1, w_in, g_q, g_k, sinks, conv_w, g_out_attn,
           g_out_conv, w_out, g_norm2, w_pq, peer_subkeys, peer_u, peer_v):
    b, s, d = x.shape
    assert d == D_MODEL and s % WINDOW == 0 and (b * s) % LANES == 0

    mod = _ada(c, w_ada, b_ada).reshape(b, 6, d)

    rc, rs1, rs2 = _rope_tables(s)
    blk = jnp.arange(MXU_DIM) // HEAD_DIM
    bd = (blk[:, None] == blk[None, :]).astype(BF16)
    x1 = _mixer(x, mod, g_norm1.reshape(1, d), w_in.astype(BF16),
                jnp.tile(g_q, N_Q_HEADS).reshape(1, ATTN_WIDTH),
                jnp.tile(g_k, N_KV_HEADS).reshape(1, KV_WIDTH),
                rc, rs1, rs2, conv_w, g_out_attn.reshape(1, ATTN_WIDTH),
                g_out_conv.reshape(1, CONV_WIDTH), w_out.astype(BF16), bd, sinks)

    a_mat = _fold(peer_subkeys.astype(BF16), w_pq.astype(BF16))
    h2, rank2, e2w, cnt, e1w = _route(x1, mod, g_norm2.reshape(1, d), a_mat)
    ec = PEER_E1 * PEER_NKEYS
    vt = peer_v.astype(BF16).reshape(PEER_N // ec, ec, d).transpose(0, 2, 1)
    yt = _peer(h2, peer_u.astype(BF16), vt, rank2, e2w, cnt, e1w)
    return _final(x1, mod, yt)
```

```python
import functools
import math

import jax
import jax.numpy as jnp
from jax import lax
from jax.experimental import pallas as pl
from jax.experimental.pallas import tpu as pltpu

F32 = jnp.float32
BF16 = jnp.bfloat16

D_MODEL = 2048
HEAD_DIM = 64
N_Q_HEADS = 16
N_KV_HEADS = 4
Q_PER_KV = N_Q_HEADS // N_KV_HEADS
ATTN_WIDTH = N_Q_HEADS * HEAD_DIM
KV_WIDTH = N_KV_HEADS * HEAD_DIM
CONV_WIDTH = D_MODEL - ATTN_WIDTH
CONV_K = 3
IN_COLS = ATTN_WIDTH + 2 * KV_WIDTH + 3 * CONV_WIDTH
WINDOW = 128
ROPE_THETA = 500000.0
ROPE_DIM = HEAD_DIM // 4
ATTN_SCALE = HEAD_DIM ** -0.5
NEG_INF = -1e30
PEER_HEADS = 8
PEER_NKEYS = 128
PEER_N = PEER_NKEYS * PEER_NKEYS
PEER_HALF = 128
PEER_TOPK = 16
EPS = 1e-6

LANES = 128
MXU_DIM = 256
VMEM_LIMIT = 56 * 1024 * 1024

ADA_COLS = 1024
MIX_ROWS = 256
ROUTE_ROWS = 256
PEER_ROWS = 512
PEER_E1 = 8
E2_ROWS = 32
SCORE_ROWS = 512
VALUE_ROWS = 512

_PAIRS = tuple((a, b) for a in range(PEER_TOPK) for b in range(PEER_TOPK)
               if (a + 1) * (b + 1) <= PEER_TOPK)


def _dot_t(a, b):
    return lax.dot_general(a, b, (((1,), (1,)), ((), ())),
                           preferred_element_type=F32)


def _group_mean_sq(y, bd):
    y2 = y * y
    hi = y2.astype(BF16)
    lo = (y2 - hi.astype(F32)).astype(BF16)
    outs = []
    for j in range(y.shape[1] // MXU_DIM):
        sl = slice(MXU_DIM * j, MXU_DIM * (j + 1))
        outs.append(jnp.dot(hi[:, sl], bd, preferred_element_type=F32)
                    + jnp.dot(lo[:, sl], bd, preferred_element_type=F32))
    ss = outs[0] if len(outs) == 1 else jnp.concatenate(outs, axis=1)
    return ss * (1.0 / HEAD_DIM)


def _rope(t, rc, rs1, rs2):
    outs = []
    for j in range(t.shape[1] // LANES):
        tj = t[:, LANES * j:LANES * (j + 1)]
        outs.append(tj * rc
                    + pltpu.roll(tj, LANES - ROPE_DIM // 2, 1) * rs1
                    + pltpu.roll(tj, ROPE_DIM // 2, 1) * rs2)
    return outs[0] if len(outs) == 1 else jnp.concatenate(outs, axis=1)


def _ada_kernel(c_ref, w_ref, b_ref, o_ref):
    c = c_ref[...]
    s = c * jax.nn.sigmoid(c)
    o_ref[...] = jnp.dot(s, w_ref[...], preferred_element_type=F32,
                         precision=lax.Precision.HIGHEST) + b_ref[...]


def _ada(c, w_ada, b_ada):
    b, d = c.shape
    n = w_ada.shape[1]
    return pl.pallas_call(
        _ada_kernel,
        grid=(n // ADA_COLS,),
        in_specs=[pl.BlockSpec((b, d), lambda j: (0, 0)),
                  pl.BlockSpec((d, ADA_COLS), lambda j: (0, j)),
                  pl.BlockSpec((1, ADA_COLS), lambda j: (0, j))],
        out_specs=pl.BlockSpec((b, ADA_COLS), lambda j: (0, j)),
        out_shape=jax.ShapeDtypeStruct((b, n), F32),
        compiler_params=pltpu.CompilerParams(
            dimension_semantics=("arbitrary",), vmem_limit_bytes=VMEM_LIMIT),
        name="ada",
    )(c, w_ada, b_ada.reshape(1, n))


def _mixer_kernel(sinks_ref, x_ref, mod_ref, g1_ref, win_ref, gq_ref, gk_ref,
                  rc_ref, rs1_ref, rs2_ref, cw_ref, goa_ref, goc_ref, wout_ref,
                  bd_ref, g2_ref, o_ref, h2_ref, kk_ref, vv_ref, u_ref, qn_ref, mixed_ref):
    ts = x_ref.shape[0]
    si = pl.program_id(1)
    x = x_ref[...]
    mod = mod_ref[...]
    sh1, sc1, gt1 = mod[0:1], mod[1:2], mod[2:3]
    bd = bd_ref[...]
    rc, rs1, rs2 = rc_ref[...], rs1_ref[...], rs2_ref[...]

    ms = jnp.mean(x * x, axis=-1, keepdims=True)
    h1 = ((x * lax.rsqrt(ms + EPS)) * g1_ref[...]) * (1.0 + sc1) + sh1
    h1b = h1.astype(BF16)

    @pl.when(si == 0)
    def _():
        kk_ref[:, :, 0:WINDOW, :] = jnp.zeros((N_KV_HEADS, 2, WINDOW, LANES), BF16)
        vv_ref[:, :, 0:WINDOW, :] = jnp.zeros((N_KV_HEADS, 2, WINDOW, LANES), BF16)
        u_ref[0:8, :] = jnp.zeros((8, CONV_WIDTH), F32)

    @pl.when(si > 0)
    def _():
        kk_ref[:, :, 0:WINDOW, :] = kk_ref[:, :, ts:ts + WINDOW, :]
        vv_ref[:, :, 0:WINDOW, :] = vv_ref[:, :, ts:ts + WINDOW, :]
        u_ref[0:8, :] = u_ref[ts:ts + 8, :]

    def proj(lo, width):
        return jnp.dot(h1b, win_ref[:, lo:lo + width], preferred_element_type=F32)

    q = proj(0, ATTN_WIDTH)
    q = q * lax.rsqrt(_group_mean_sq(q, bd) + EPS) * gq_ref[...]
    qn_ref[...] = (_rope(q, rc, rs1, rs2) * ATTN_SCALE).astype(BF16)

    k = proj(ATTN_WIDTH, KV_WIDTH)
    k = k * lax.rsqrt(_group_mean_sq(k, bd) + EPS) * gk_ref[...]
    k = _rope(k, rc, rs1, rs2)
    v = proj(ATTN_WIDTH + KV_WIDTH, KV_WIDTH)
    low = lax.broadcasted_iota(jnp.int32, (1, LANES), 1) < HEAD_DIM
    for g in range(N_KV_HEADS):
        col = slice(LANES * (g // 2), LANES * (g // 2 + 1))
        for t, dst in ((k[:, col], kk_ref), (v[:, col], vv_ref)):
            tr = pltpu.roll(t, HEAD_DIM, 1)
            in_low, in_high = (t, tr) if g % 2 == 0 else (tr, t)
            dst[g, 0, WINDOW:WINDOW + ts, :] = jnp.where(low, in_low, 0.0).astype(BF16)
            dst[g, 1, WINDOW:WINDOW + ts, :] = jnp.where(low, 0.0, in_high).astype(BF16)

    qi = lax.broadcasted_iota(jnp.int32, (WINDOW, 2 * WINDOW), 0)
    kj = lax.broadcasted_iota(jnp.int32, (WINDOW, 2 * WINDOW), 1)
    band = (kj > qi) & (kj <= qi + WINDOW)

    def scores(jb, g):
        qrows = slice(jb * WINDOW, (jb + 1) * WINDOW)
        krows = slice(jb * WINDOW, (jb + 2) * WINDOW)
        qa = qn_ref[qrows, MXU_DIM * g:MXU_DIM * g + LANES]
        qb = qn_ref[qrows, MXU_DIM * g + LANES:MXU_DIM * (g + 1)]
        q2 = jnp.concatenate([qa, qb], axis=0)
        return (_dot_t(q2, kk_ref[g, 0, krows, :]),
                _dot_t(q2, kk_ref[g, 1, krows, :]))

    def attend(jb, g, s_lo, s_hi):
        first_key = jnp.where(si * (ts // WINDOW) + jb == 0, WINDOW, 0)
        mask = band & (kj >= first_key)
        qrows = slice(jb * WINDOW, (jb + 1) * WINDOW)
        krows = slice(jb * WINDOW, (jb + 2) * WINDOW)
        vcat = jnp.concatenate([vv_ref[g, 0, krows, :], vv_ref[g, 1, krows, :]], axis=0)
        halves = []
        for half in range(2):
            rows = slice(half * WINDOW, (half + 1) * WINDOW)
            ps, inv = [], []
            for o, s in enumerate((s_lo[rows], s_hi[rows])):
                sink = sinks_ref[Q_PER_KV * g + 2 * half + o]
                s = jnp.where(mask, s, NEG_INF)
                m = jnp.maximum(jnp.max(s, axis=-1, keepdims=True), sink)
                p = jnp.exp(s - m)
                denom = jnp.sum(p, axis=-1, keepdims=True) + jnp.exp(sink - m)
                ps.append(p.astype(BF16))
                inv.append(1.0 / denom)
            pv = jnp.dot(jnp.concatenate(ps, axis=1), vcat, preferred_element_type=F32)
            halves.append(pv * jnp.where(low, inv[0], inv[1]))
        att = jnp.concatenate(halves, axis=1)
        cols = slice(MXU_DIM * g, MXU_DIM * (g + 1))
        att = att * lax.rsqrt(_group_mean_sq(att, bd) + EPS) * goa_ref[:, cols]
        mixed_ref[qrows, cols] = att.astype(BF16)

    o0 = ATTN_WIDTH + 2 * KV_WIDTH
    cw = cw_ref[...]

    def conv_block(cb):
        cols = slice(cb * MXU_DIM, (cb + 1) * MXU_DIM)
        bg = proj(o0 + cb * MXU_DIM, MXU_DIM)
        u_ref[8:8 + ts, cols] = (proj(o0 + CONV_WIDTH + cb * MXU_DIM, MXU_DIM)
                                 * proj(o0 + 2 * CONV_WIDTH + cb * MXU_DIM, MXU_DIM))
        acc = cw[0:1, cols] * u_ref[6:6 + ts, cols]
        acc = acc + cw[1:2, cols] * u_ref[7:7 + ts, cols]
        acc = acc + cw[2:3, cols] * u_ref[8:8 + ts, cols]
        y = bg * acc
        y = y * lax.rsqrt(_group_mean_sq(y, bd) + EPS) * goc_ref[:, cols]
        mixed_ref[:, ATTN_WIDTH + cb * MXU_DIM:ATTN_WIDTH + (cb + 1) * MXU_DIM] = y.astype(BF16)

    iters = [(jb, g) for jb in range(ts // WINDOW) for g in range(N_KV_HEADS)]
    n_conv = CONV_WIDTH // MXU_DIM
    conv_done = 0
    s_cur = scores(*iters[0])
    for i, (jb, g) in enumerate(iters):
        s_next = scores(*iters[i + 1]) if i + 1 < len(iters) else None
        want = -(-n_conv * (i + 1) // len(iters))
        while conv_done < want:
            conv_block(conv_done)
            conv_done += 1
        attend(jb, g, *s_cur)
        s_cur = s_next

    x1 = x + gt1 * jnp.dot(mixed_ref[...], wout_ref[...], preferred_element_type=F32)
    o_ref[...] = x1

    sh2, sc2 = mod[3:4], mod[4:5]
    ms2 = jnp.mean(x1 * x1, axis=-1, keepdims=True)
    h2 = ((x1 * lax.rsqrt(ms2 + EPS)) * g2_ref[...]) * (1.0 + sc2) + sh2
    h2_ref[...] = h2.astype(BF16)


def _mixer(x, mod, g_norm1, w_in_b, gq, gk, rc, rs1, rs2, conv_w, goa, goc, w_out_b, bd, sinks,
           g_norm2):
    b, s, d = x.shape
    ts = min(MIX_ROWS, s)
    const = lambda shape: pl.BlockSpec(shape, lambda bi, si: (0,) * len(shape),
                                       pipeline_mode=pl.Buffered(1))
    rope_spec = pl.BlockSpec((ts, LANES), lambda bi, si: (si, 0))
    return pl.pallas_call(
        _mixer_kernel,
        grid=(b, s // ts),
        in_specs=[pl.BlockSpec(memory_space=pltpu.SMEM),
                  pl.BlockSpec((None, ts, d), lambda bi, si: (bi, si, 0)),
                  pl.BlockSpec((None, 6, d), lambda bi, si: (bi, 0, 0)),
                  const((1, d)),
                  const((d, IN_COLS)),
                  const((1, ATTN_WIDTH)),
                  const((1, KV_WIDTH)),
                  rope_spec, rope_spec, rope_spec,
                  const((CONV_K, CONV_WIDTH)),
                  const((1, ATTN_WIDTH)),
                  const((1, CONV_WIDTH)),
                  const((d, d)),
                  const((MXU_DIM, MXU_DIM)),
                  const((1, d))],
        out_specs=[pl.BlockSpec((None, ts, d), lambda bi, si: (bi, si, 0)),
                   pl.BlockSpec((None, ts, d), lambda bi, si: (bi, si, 0))],
        out_shape=[jax.ShapeDtypeStruct((b, s, d), F32), jax.ShapeDtypeStruct((b, s, d), BF16)],
        scratch_shapes=[pltpu.VMEM((N_KV_HEADS, 2, ts + WINDOW, LANES), BF16),
                        pltpu.VMEM((N_KV_HEADS, 2, ts + WINDOW, LANES), BF16),
                        pltpu.VMEM((ts + 8, CONV_WIDTH), F32),
                        pltpu.VMEM((ts, ATTN_WIDTH), BF16),
                        pltpu.VMEM((ts, d), BF16)],
        compiler_params=pltpu.CompilerParams(
            dimension_semantics=("arbitrary", "arbitrary"), vmem_limit_bytes=VMEM_LIMIT),
        name="mixer",
    )(sinks, x, mod, g_norm1, w_in_b, gq, gk, rc, rs1, rs2, conv_w, goa, goc, w_out_b, bd, g_norm2)


def _fold_kernel(sk_ref, w_ref, o_ref):
    o_ref[...] = _dot_t(sk_ref[...], w_ref[...]).astype(BF16)


def _fold(subkeys_b, w_pq_b):
    d = w_pq_b.shape[0]
    return pl.pallas_call(
        _fold_kernel,
        grid=(2, PEER_HEADS),
        in_specs=[pl.BlockSpec((None, None, PEER_NKEYS, PEER_HALF), lambda p, h: (h, p, 0, 0)),
                  pl.BlockSpec((d, PEER_HALF), lambda p, h: (0, 2 * h + p))],
        out_specs=pl.BlockSpec((PEER_NKEYS, d), lambda p, h: (p * PEER_HEADS + h, 0)),
        out_shape=jax.ShapeDtypeStruct((2 * PEER_HEADS * PEER_NKEYS, d), BF16),
        compiler_params=pltpu.CompilerParams(
            dimension_semantics=("arbitrary", "arbitrary"), vmem_limit_bytes=VMEM_LIMIT),
        name="fold",
    )(subkeys_b, w_pq_b)


def _compare_exchange(v, i, l, descending):
    hi, lo = jnp.maximum(v[i], v[l]), jnp.minimum(v[i], v[l])
    v[i], v[l] = (hi, lo) if descending else (lo, hi)


def _bitonic_merge(v):
    n = len(v)
    j = n // 2
    while j >= 1:
        for i in range(n):
            if i ^ j > i:
                _compare_exchange(v, i, i ^ j, True)
        j //= 2
    return v


def _top16(s, want_rank=False):
    n = PEER_TOPK
    sub = s.shape[0] // n
    v = [s[sub * j:sub * (j + 1), :] for j in range(n)]
    k = 2
    while k <= n:
        j = k // 2
        while j >= 1:
            for i in range(n):
                if i ^ j > i:
                    _compare_exchange(v, i, i ^ j, (i & k) == 0)
            j //= 2
        k *= 2
    shift = sub // 2
    while shift >= 1:
        rolled = [pltpu.roll(w, shift, 0) for w in v]
        v = _bitonic_merge([jnp.maximum(v[i], rolled[n - 1 - i]) for i in range(n)])
        shift //= 2
    t = [w[0:1, :] for w in v]
    if not want_rank:
        return t
    b8 = t[7] > s
    b4 = jnp.where(b8, t[11], t[3]) > s
    b2 = jnp.where(b8, jnp.where(b4, t[13], t[9]), jnp.where(b4, t[5], t[1])) > s
    b1 = jnp.where(b8,
                   jnp.where(b4, jnp.where(b2, t[14], t[12]), jnp.where(b2, t[10], t[8])),
                   jnp.where(b4, jnp.where(b2, t[6], t[4]), jnp.where(b2, t[2], t[0]))) > s
    rank = (jnp.where(b8, 8.0, 0.0) + jnp.where(b4, 4.0, 0.0) + jnp.where(b2, 2.0, 0.0)
            + jnp.where(b1, 1.0, 0.0) + jnp.where(t[15] > s, 1.0, 0.0))
    return t, rank


def _route_kernel(h2_ref, a_ref, rank2_ref, e2w_ref, cnt_ref, e1w_ref, st0_ref, st1_ref):
    i = pl.program_id(0)

    @pl.when(i == 0)
    def _():
        st1_ref[...] = jnp.zeros(st1_ref.shape, F32)

    hk = PEER_HEADS * PEER_NKEYS
    tr = h2_ref.shape[0]

    def run(st_new, st_old):
        thunks = []
        for g in range(tr // LANES):
            lanes = slice(g * LANES, (g + 1) * LANES)
            t1, t2 = [None] * PEER_HEADS, [None] * PEER_HEADS
            state = {}

            def top1(h, lanes=lanes, t1=t1):
                t1[h] = _top16(st_old[h * PEER_NKEYS:(h + 1) * PEER_NKEYS, lanes])

            def top2(h, g=g, lanes=lanes, t2=t2):
                vals, rank = _top16(st_old[hk + h * PEER_NKEYS:hk + (h + 1) * PEER_NKEYS, lanes], True)
                t2[h] = vals
                rank2_ref[g, h] = rank.astype(BF16)

            def candidates(t1=t1, t2=t2, state=state):
                r1 = [jnp.concatenate([t1[h][a] for h in range(PEER_HEADS)], axis=0)
                      for a in range(PEER_TOPK)]
                r2 = [jnp.concatenate([t2[h][a] for h in range(PEER_HEADS)], axis=0)
                      for a in range(PEER_TOPK)]
                zs = [r1[a] + r2[b] for (a, b) in _PAIRS]
                cur = list(zs)
                z16 = None
                for it in range(PEER_TOPK):
                    z16 = functools.reduce(jnp.maximum, cur)
                    if it + 1 < PEER_TOPK:
                        cur = [jnp.where(c == z16, -jnp.inf, c) for c in cur]
                zmax = zs[0]
                zsum = jnp.zeros_like(zmax)
                cnt = [jnp.zeros_like(zmax) for _ in range(PEER_TOPK)]
                for (a, b), z in zip(_PAIRS, zs):
                    sel = z >= z16
                    zsum = zsum + jnp.where(sel, jnp.exp(z - zmax), 0.0)
                    cnt[a] = cnt[a] + jnp.where(sel, 1.0, 0.0)
                state["cnt"] = cnt
                state["inv_z"] = 1.0 / zsum

            def finish(h, g=g, lanes=lanes, t1=t1, t2=t2, state=state):
                s1 = st_old[h * PEER_NKEYS:(h + 1) * PEER_NKEYS, lanes]
                s2 = st_old[hk + h * PEER_NKEYS:hk + (h + 1) * PEER_NKEYS, lanes]
                cnt_full = jnp.zeros_like(s1)
                for a in range(PEER_TOPK):
                    cnt_full = jnp.where(s1 == t1[h][a], state["cnt"][a][h:h + 1, :], cnt_full)
                e2w_ref[g, h] = (jnp.exp(s2 - t2[h][0]) * state["inv_z"][h:h + 1, :]).astype(BF16)
                cnt_ref[g, h] = cnt_full
                e1w_ref[g, h] = jnp.exp(s1 - t1[h][0])

            for h in range(PEER_HEADS):
                thunks.append(functools.partial(top1, h))
                thunks.append(functools.partial(top2, h))
            thunks.append(candidates)
            for h in range(PEER_HEADS):
                thunks.append(functools.partial(finish, h))

        rows_per = 512
        n_rb, n_k = st_new.shape[0] // rows_per, h2_ref.shape[1] // MXU_DIM
        n_dots, done_dots, done_thunks = n_rb * n_k, 0, 0
        for rb in range(n_rb):
            rows = slice(rb * rows_per, (rb + 1) * rows_per)
            acc = None
            for k in range(n_k):
                ks = slice(k * MXU_DIM, (k + 1) * MXU_DIM)
                part = _dot_t(a_ref[rows, ks], h2_ref[:, ks])
                acc = part if acc is None else acc + part
                done_dots += 1
                target = -(-len(thunks) * done_dots // n_dots)
                while done_thunks < target:
                    thunks[done_thunks]()
                    done_thunks += 1
            st_new[rows, :] = acc

    @pl.when(i % 2 == 0)
    def _():
        run(st0_ref, st1_ref)

    @pl.when(i % 2 == 1)
    def _():
        run(st1_ref, st0_ref)


def _route(h2, a_mat, seq):
    n, d = h2.shape
    tr = min(ROUTE_ROWS, seq)
    grp = tr // LANES
    tiles = n // tr
    cur = lambda i: jnp.minimum(i, tiles - 1)
    prev = lambda i: jnp.maximum(i - 1, 0)
    rshape = (n // LANES, PEER_HEADS, PEER_NKEYS, LANES)
    rspec = pl.BlockSpec((grp, PEER_HEADS, PEER_NKEYS, LANES), lambda i: (prev(i), 0, 0, 0))
    return pl.pallas_call(
        _route_kernel,
        grid=(tiles + 1,),
        in_specs=[pl.BlockSpec((tr, d), lambda i: (cur(i), 0)),
                  pl.BlockSpec(a_mat.shape, lambda i: (0, 0), pipeline_mode=pl.Buffered(1))],
        out_specs=[rspec, rspec, rspec, rspec],
        out_shape=[jax.ShapeDtypeStruct(rshape, BF16), jax.ShapeDtypeStruct(rshape, BF16),
                   jax.ShapeDtypeStruct(rshape, F32), jax.ShapeDtypeStruct(rshape, F32)],
        scratch_shapes=[pltpu.VMEM((a_mat.shape[0], tr), F32), pltpu.VMEM((a_mat.shape[0], tr), F32)],
        compiler_params=pltpu.CompilerParams(
            dimension_semantics=("arbitrary",), vmem_limit_bytes=VMEM_LIMIT),
        name="route",
    )(h2, a_mat)


def _peer_kernel(h2_ref, u_ref, vt_ref, rank2_ref, e2w_ref, cnt_ref, e1w_ref, yt_ref,
                 a0_scr, a1_scr, act0_scr, act1_scr):
    g = pl.program_id(0)
    groups, ec = a0_scr.shape[0], a0_scr.shape[1]
    d = yt_ref.shape[0]
    n_chunks = PEER_NKEYS // PEER_E1

    @pl.when(g == 0)
    def _():
        for ref in (a0_scr, a1_scr, act0_scr, act1_scr):
            ref[...] = jnp.zeros(ref.shape, ref.dtype)

    @pl.when((g == 0) | ((g >= 2) & ((g - 2) % n_chunks == 0)))
    def _():
        yt_ref[...] = jnp.zeros(yt_ref.shape, F32)

    sqrt_half = math.sqrt(0.5)

    def run(a_new, a_old, act_new, act_old):
        nblk = PEER_NKEYS // E2_ROWS

        def gate_thunks(e1l, tb):
            w = [jnp.zeros((E2_ROWS, LANES), BF16) for _ in range(nblk)]

            def head_part(h):
                cnt = jnp.broadcast_to(cnt_ref[tb, h, e1l:e1l + 1, :], (E2_ROWS, LANES)).astype(BF16)
                w1 = jnp.broadcast_to(e1w_ref[tb, h, e1l:e1l + 1, :], (E2_ROWS, LANES)).astype(BF16)
                for eb in range(nblk):
                    e2rows = slice(eb * E2_ROWS, (eb + 1) * E2_ROWS)
                    w[eb] = w[eb] + w1 * jnp.where(rank2_ref[tb, h, e2rows, :] < cnt,
                                                   e2w_ref[tb, h, e2rows, :], jnp.zeros((), BF16))

            def finish_part(eb):
                rows = slice(e1l * PEER_NKEYS + eb * E2_ROWS,
                             e1l * PEER_NKEYS + (eb + 1) * E2_ROWS)
                a = a_old[tb, rows, :]
                gelu = 0.5 * a * (1.0 + lax.erf(a * sqrt_half))
                lanes = slice((tb % 2) * LANES, (tb % 2 + 1) * LANES)
                act_new[tb // 2, rows, lanes] = gelu.astype(BF16) * w[eb]

            return ([functools.partial(head_part, h) for h in range(PEER_HEADS)]
                    + [functools.partial(finish_part, eb) for eb in range(nblk)])

        thunks = [t for e1l in range(PEER_E1) for tb in range(groups) for t in gate_thunks(e1l, tb)]

        halves = groups // 2
        kt_score, kt_value = d // MXU_DIM, ec // MXU_DIM
        score_rows, value_rows = min(SCORE_ROWS, ec), min(VALUE_ROWS, d)
        mxu_ops = [("score", rb, nb) for rb in range(ec // score_rows) for nb in range(halves)]
        mxu_ops += [("value", rb, nb) for rb in range(d // value_rows) for nb in range(halves)]
        n_small = halves * ((ec // MXU_DIM) * kt_score + (d // MXU_DIM) * kt_value)
        done_small, done_thunks = 0, 0

        def vpu_fill():
            nonlocal done_thunks
            target = -(-len(thunks) * done_small // n_small)
            while done_thunks < min(target, len(thunks)):
                thunks[done_thunks]()
                done_thunks += 1

        for kind, rb, nb in mxu_ops:
            tok = slice(nb * MXU_DIM, (nb + 1) * MXU_DIM)
            acc = None
            if kind == "score":
                rows = slice(rb * score_rows, (rb + 1) * score_rows)
                for k in range(kt_score):
                    ks = slice(k * MXU_DIM, (k + 1) * MXU_DIM)
                    part = _dot_t(u_ref[rows, ks], h2_ref[tok, ks])
                    acc = part if acc is None else acc + part
                    done_small += score_rows // MXU_DIM
                    vpu_fill()
                a_new[2 * nb, rows, :] = acc[:, :LANES]
                a_new[2 * nb + 1, rows, :] = acc[:, LANES:]
            else:
                rows = slice(rb * value_rows, (rb + 1) * value_rows)
                for k in range(kt_value):
                    ks = slice(k * MXU_DIM, (k + 1) * MXU_DIM)
                    part = jnp.dot(vt_ref[rows, ks], act_old[nb, ks, :], preferred_element_type=F32)
                    acc = part if acc is None else acc + part
                    done_small += value_rows // MXU_DIM
                    vpu_fill()
                yt_ref[rows, tok] += acc
        assert done_thunks == len(thunks)

    @pl.when(g % 2 == 0)
    def _():
        run(a0_scr, a1_scr, act1_scr, act0_scr)

    @pl.when(g % 2 == 1)
    def _():
        run(a1_scr, a0_scr, act0_scr, act1_scr)


def _peer(h2, u_b, vt_b, rank2, e2w, cnt, e1w):
    n, d = h2.shape
    tp = min(PEER_ROWS, n)
    grp = tp // LANES
    ec = PEER_E1 * PEER_NKEYS
    nc = PEER_N // ec
    total = (n // tp) * nc
    lag = lambda g, k: jnp.clip(g - k, 0, total - 1)
    full = pl.BlockSpec((grp, PEER_HEADS, PEER_NKEYS, LANES), lambda g: (lag(g, 1) // nc, 0, 0, 0))
    part = pl.BlockSpec((grp, PEER_HEADS, PEER_E1, LANES),
                        lambda g: (lag(g, 1) // nc, 0, lag(g, 1) % nc, 0))
    return pl.pallas_call(
        _peer_kernel,
        grid=(total + 2,),
        in_specs=[pl.BlockSpec((tp, d), lambda g: (lag(g, 0) // nc, 0)),
                  pl.BlockSpec((ec, d), lambda g: (lag(g, 0) % nc, 0)),
                  pl.BlockSpec((None, d, ec), lambda g: (lag(g, 2) % nc, 0, 0)),
                  full, full, part, part],
        out_specs=pl.BlockSpec((d, tp), lambda g: (0, lag(g, 2) // nc)),
        out_shape=jax.ShapeDtypeStruct((d, n), F32),
        scratch_shapes=[pltpu.VMEM((grp, ec, LANES), F32), pltpu.VMEM((grp, ec, LANES), F32),
                        pltpu.VMEM((grp // 2, ec, MXU_DIM), BF16),
                        pltpu.VMEM((grp // 2, ec, MXU_DIM), BF16)],
        compiler_params=pltpu.CompilerParams(
            dimension_semantics=("arbitrary",), vmem_limit_bytes=VMEM_LIMIT),
        name="peer",
    )(h2, u_b, vt_b, rank2, e2w, cnt, e1w)


def _final_kernel(x_ref, mod_ref, yt_ref, o_ref):
    gt2 = mod_ref[...][5:6]
    o_ref[...] = x_ref[...] + gt2 * yt_ref[...].T


def _final(x1, mod, yt):
    b, s, d = x1.shape
    tf = min(PEER_ROWS, s)
    per_seq = s // tf
    return pl.pallas_call(
        _final_kernel,
        grid=(b, per_seq),
        in_specs=[pl.BlockSpec((None, tf, d), lambda bi, si: (bi, si, 0)),
                  pl.BlockSpec((None, 6, d), lambda bi, si: (bi, 0, 0)),
                  pl.BlockSpec((d, tf), lambda bi, si: (0, bi * per_seq + si))],
        out_specs=pl.BlockSpec((None, tf, d), lambda bi, si: (bi, si, 0)),
        out_shape=jax.ShapeDtypeStruct((b, s, d), F32),
        compiler_params=pltpu.CompilerParams(
            dimension_semantics=("arbitrary", "arbitrary"), vmem_limit_bytes=VMEM_LIMIT),
        name="final",
    )(x1, mod, yt)


def _rope_tables(s):
    pos = jnp.arange(s, dtype=F32)
    inv_freq = ROPE_THETA ** (-jnp.arange(0, ROPE_DIM, 2, dtype=F32) / ROPE_DIM)
    ang = pos[:, None] * inv_freq[None, :]
    cos, sin = jnp.cos(ang), jnp.sin(ang)
    half = ROPE_DIM // 2
    ones = jnp.ones((s, HEAD_DIM - ROPE_DIM), F32)
    zeros = jnp.zeros((s, HEAD_DIM - ROPE_DIM), F32)
    zh = jnp.zeros((s, half), F32)
    rc = jnp.concatenate([cos, cos, ones], axis=1)
    rs1 = jnp.concatenate([-sin, zh, zeros], axis=1)
    rs2 = jnp.concatenate([zh, sin, zeros], axis=1)
    tile = lambda t: jnp.tile(t, (1, LANES // HEAD_DIM))
    return tile(rc), tile(rs1), tile(rs2)


def kernel(x, c, w_ada, b_ada, g_norm1, w_in, g_q, g_k, sinks, conv_w, g_out_attn,
           g_out_conv, w_out, g_norm2, w_pq, peer_subkeys, peer_u, peer_v):
    b, s, d = x.shape
    assert d == D_MODEL and s % WINDOW == 0 and (b * s) % LANES == 0

    mod = _ada(c, w_ada, b_ada).reshape(b, 6, d)

    rc, rs1, rs2 = _rope_tables(s)
    blk = jnp.arange(MXU_DIM) // HEAD_DIM
    bd = (blk[:, None] == blk[None, :]).astype(BF16)
    x1, h2 = _mixer(x, mod, g_norm1.reshape(1, d), w_in.astype(BF16),
                jnp.tile(g_q, N_Q_HEADS).reshape(1, ATTN_WIDTH),
                jnp.tile(g_k, N_KV_HEADS).reshape(1, KV_WIDTH),
                rc, rs1, rs2, conv_w, g_out_attn.reshape(1, ATTN_WIDTH),
                g_out_conv.reshape(1, CONV_WIDTH), w_out.astype(BF16), bd, sinks,
                g_norm2.reshape(1, d))

    a_mat = _fold(peer_subkeys.astype(BF16), w_pq.astype(BF16))
    h2 = h2.reshape(b * s, d)
    rank2, e2w, cnt, e1w = _route(h2, a_mat, s)
    ec = PEER_E1 * PEER_NKEYS
    vt = peer_v.astype(BF16).reshape(PEER_N // ec, ec, d).transpose(0, 2, 1)
    yt = _peer(h2, peer_u.astype(BF16), vt, rank2, e2w, cnt, e1w)
    return _final(x1, mod, yt)
```

```python
import functools
import math

import jax
import jax.numpy as jnp
from jax import lax
from jax.experimental import pallas as pl
from jax.experimental.pallas import tpu as pltpu

F32 = jnp.float32
BF16 = jnp.bfloat16

D_MODEL = 2048
HEAD_DIM = 64
N_Q_HEADS = 16
N_KV_HEADS = 4
Q_PER_KV = N_Q_HEADS // N_KV_HEADS
ATTN_WIDTH = N_Q_HEADS * HEAD_DIM
KV_WIDTH = N_KV_HEADS * HEAD_DIM
CONV_WIDTH = D_MODEL - ATTN_WIDTH
CONV_K = 3
IN_COLS = ATTN_WIDTH + 2 * KV_WIDTH + 3 * CONV_WIDTH
WINDOW = 128
ROPE_THETA = 500000.0
ROPE_DIM = HEAD_DIM // 4
ATTN_SCALE = HEAD_DIM ** -0.5
NEG_INF = -1e30
PEER_HEADS = 8
PEER_NKEYS = 128
PEER_N = PEER_NKEYS * PEER_NKEYS
PEER_HALF = 128
PEER_TOPK = 16
EPS = 1e-6

LANES = 128
MXU_DIM = 256
VMEM_LIMIT = 56 * 1024 * 1024

ADA_COLS = 1024
MIX_ROWS = 256
ROUTE_ROWS = 256
PEER_ROWS = 512
PEER_E1 = 8
E2_ROWS = 32
SCORE_ROWS = 512
VALUE_ROWS = 512

_PAIRS = tuple((a, b) for a in range(PEER_TOPK) for b in range(PEER_TOPK)
               if (a + 1) * (b + 1) <= PEER_TOPK)


def _dot_t(a, b):
    return lax.dot_general(a, b, (((1,), (1,)), ((), ())),
                           preferred_element_type=F32)


def _group_mean_sq(y, bd):
    y2 = y * y
    hi = y2.astype(BF16)
    lo = (y2 - hi.astype(F32)).astype(BF16)
    outs = []
    for j in range(y.shape[1] // MXU_DIM):
        sl = slice(MXU_DIM * j, MXU_DIM * (j + 1))
        outs.append(jnp.dot(hi[:, sl], bd, preferred_element_type=F32)
                    + jnp.dot(lo[:, sl], bd, preferred_element_type=F32))
    ss = outs[0] if len(outs) == 1 else jnp.concatenate(outs, axis=1)
    return ss * (1.0 / HEAD_DIM)


def _rope(t, rc, rs1, rs2):
    outs = []
    for j in range(t.shape[1] // LANES):
        tj = t[:, LANES * j:LANES * (j + 1)]
        outs.append(tj * rc
                    + pltpu.roll(tj, LANES - ROPE_DIM // 2, 1) * rs1
                    + pltpu.roll(tj, ROPE_DIM // 2, 1) * rs2)
    return outs[0] if len(outs) == 1 else jnp.concatenate(outs, axis=1)


def _ada_kernel(c_ref, w_ref, b_ref, o_ref):
    c = c_ref[...]
    s = c * jax.nn.sigmoid(c)
    o_ref[...] = jnp.dot(s, w_ref[...], preferred_element_type=F32,
                         precision=lax.Precision.HIGHEST) + b_ref[...]


def _ada(c, w_ada, b_ada):
    b, d = c.shape
    n = w_ada.shape[1]
    return pl.pallas_call(
        _ada_kernel,
        grid=(n // ADA_COLS,),
        in_specs=[pl.BlockSpec((b, d), lambda j: (0, 0)),
                  pl.BlockSpec((d, ADA_COLS), lambda j: (0, j)),
                  pl.BlockSpec((1, ADA_COLS), lambda j: (0, j))],
        out_specs=pl.BlockSpec((b, ADA_COLS), lambda j: (0, j)),
        out_shape=jax.ShapeDtypeStruct((b, n), F32),
        compiler_params=pltpu.CompilerParams(
            dimension_semantics=("arbitrary",), vmem_limit_bytes=VMEM_LIMIT),
        name="ada",
    )(c, w_ada, b_ada.reshape(1, n))


def _mixer_kernel(sinks_ref, x_ref, mod_ref, g1_ref, win_ref, gq_ref, gk_ref,
                  rc_ref, rs1_ref, rs2_ref, cw_ref, goa_ref, goc_ref, wout_ref,
                  bd_ref, g2_ref, o_ref, h2_ref, kk_ref, vv_ref, u_ref, qn_ref, mixed_ref):
    ts = x_ref.shape[0]
    si = pl.program_id(1)
    x = x_ref[...]
    mod = mod_ref[...]
    sh1, sc1, gt1 = mod[0:1], mod[1:2], mod[2:3]
    bd = bd_ref[...]
    rc, rs1, rs2 = rc_ref[...], rs1_ref[...], rs2_ref[...]

    @pl.when(si == 0)
    def _():
        kk_ref[:, :, 0:WINDOW, :] = jnp.zeros((N_KV_HEADS, 2, WINDOW, LANES), BF16)
        vv_ref[:, :, 0:WINDOW, :] = jnp.zeros((N_KV_HEADS, 2, WINDOW, LANES), BF16)
        u_ref[0:8, :] = jnp.zeros((8, CONV_WIDTH), F32)

    @pl.when(si > 0)
    def _():
        kk_ref[:, :, 0:WINDOW, :] = kk_ref[:, :, ts:ts + WINDOW, :]
        vv_ref[:, :, 0:WINDOW, :] = vv_ref[:, :, ts:ts + WINDOW, :]
        u_ref[0:8, :] = u_ref[ts:ts + 8, :]

    ms = jnp.mean(x * x, axis=-1, keepdims=True)
    h1 = ((x * lax.rsqrt(ms + EPS)) * g1_ref[...]) * (1.0 + sc1) + sh1
    h1b = h1.astype(BF16)

    def proj(lo, width):
        return jnp.dot(h1b, win_ref[:, lo:lo + width], preferred_element_type=F32)

    q = proj(0, ATTN_WIDTH)
    q = q * lax.rsqrt(_group_mean_sq(q, bd) + EPS) * gq_ref[...]
    qn_ref[...] = (_rope(q, rc, rs1, rs2) * ATTN_SCALE).astype(BF16)

    k = proj(ATTN_WIDTH, KV_WIDTH)
    k = k * lax.rsqrt(_group_mean_sq(k, bd) + EPS) * gk_ref[...]
    k = _rope(k, rc, rs1, rs2)
    v = proj(ATTN_WIDTH + KV_WIDTH, KV_WIDTH)
    low = lax.broadcasted_iota(jnp.int32, (1, LANES), 1) < HEAD_DIM
    for g in range(N_KV_HEADS):
        col = slice(LANES * (g // 2), LANES * (g // 2 + 1))
        for t, dst in ((k[:, col], kk_ref), (v[:, col], vv_ref)):
            tr = pltpu.roll(t, HEAD_DIM, 1)
            in_low, in_high = (t, tr) if g % 2 == 0 else (tr, t)
            dst[g, 0, WINDOW:WINDOW + ts, :] = jnp.where(low, in_low, 0.0).astype(BF16)
            dst[g, 1, WINDOW:WINDOW + ts, :] = jnp.where(low, 0.0, in_high).astype(BF16)

    qi = lax.broadcasted_iota(jnp.int32, (WINDOW, 2 * WINDOW), 0)
    kj = lax.broadcasted_iota(jnp.int32, (WINDOW, 2 * WINDOW), 1)
    band = (kj > qi) & (kj <= qi + WINDOW)

    def scores(jb, g):
        qrows = slice(jb * WINDOW, (jb + 1) * WINDOW)
        krows = slice(jb * WINDOW, (jb + 2) * WINDOW)
        qa = qn_ref[qrows, MXU_DIM * g:MXU_DIM * g + LANES]
        qb = qn_ref[qrows, MXU_DIM * g + LANES:MXU_DIM * (g + 1)]
        q2 = jnp.concatenate([qa, qb], axis=0)
        return (_dot_t(q2, kk_ref[g, 0, krows, :]),
                _dot_t(q2, kk_ref[g, 1, krows, :]))

    def attend(jb, g, s_lo, s_hi):
        first_key = jnp.where(si * (ts // WINDOW) + jb == 0, WINDOW, 0)
        mask = band & (kj >= first_key)
        qrows = slice(jb * WINDOW, (jb + 1) * WINDOW)
        krows = slice(jb * WINDOW, (jb + 2) * WINDOW)
        vcat = jnp.concatenate([vv_ref[g, 0, krows, :], vv_ref[g, 1, krows, :]], axis=0)
        halves = []
        for half in range(2):
            rows = slice(half * WINDOW, (half + 1) * WINDOW)
            ps, inv = [], []
            for o, s in enumerate((s_lo[rows], s_hi[rows])):
                sink = sinks_ref[Q_PER_KV * g + 2 * half + o]
                s = jnp.where(mask, s, NEG_INF)
                m = jnp.maximum(jnp.max(s, axis=-1, keepdims=True), sink)
                p = jnp.exp(s - m)
                denom = jnp.sum(p, axis=-1, keepdims=True) + jnp.exp(sink - m)
                ps.append(p.astype(BF16))
                inv.append(1.0 / denom)
            pv = jnp.dot(jnp.concatenate(ps, axis=1), vcat, preferred_element_type=F32)
            halves.append(pv * jnp.where(low, inv[0], inv[1]))
        att = jnp.concatenate(halves, axis=1)
        cols = slice(MXU_DIM * g, MXU_DIM * (g + 1))
        att = att * lax.rsqrt(_group_mean_sq(att, bd) + EPS) * goa_ref[:, cols]
        mixed_ref[qrows, cols] = att.astype(BF16)

    o0 = ATTN_WIDTH + 2 * KV_WIDTH
    cw = cw_ref[...]

    def conv_block(cb):
        cols = slice(cb * MXU_DIM, (cb + 1) * MXU_DIM)
        bg = proj(o0 + cb * MXU_DIM, MXU_DIM)
        u_ref[8:8 + ts, cols] = (proj(o0 + CONV_WIDTH + cb * MXU_DIM, MXU_DIM)
                                 * proj(o0 + 2 * CONV_WIDTH + cb * MXU_DIM, MXU_DIM))
        acc = cw[0:1, cols] * u_ref[6:6 + ts, cols]
        acc = acc + cw[1:2, cols] * u_ref[7:7 + ts, cols]
        acc = acc + cw[2:3, cols] * u_ref[8:8 + ts, cols]
        y = bg * acc
        y = y * lax.rsqrt(_group_mean_sq(y, bd) + EPS) * goc_ref[:, cols]
        mixed_ref[:, ATTN_WIDTH + cb * MXU_DIM:ATTN_WIDTH + (cb + 1) * MXU_DIM] = y.astype(BF16)

    iters = [(jb, g) for jb in range(ts // WINDOW) for g in range(N_KV_HEADS)]
    n_conv = CONV_WIDTH // MXU_DIM
    conv_done = 0
    s_cur = scores(*iters[0])
    for i, (jb, g) in enumerate(iters):
        s_next = scores(*iters[i + 1]) if i + 1 < len(iters) else None
        want = -(-n_conv * (i + 1) // len(iters))
        while conv_done < want:
            conv_block(conv_done)
            conv_done += 1
        attend(jb, g, *s_cur)
        s_cur = s_next

    x1 = x + gt1 * jnp.dot(mixed_ref[...], wout_ref[...], preferred_element_type=F32)
    o_ref[...] = x1

    sh2, sc2 = mod[3:4], mod[4:5]
    ms2 = jnp.mean(x1 * x1, axis=-1, keepdims=True)
    h2 = ((x1 * lax.rsqrt(ms2 + EPS)) * g2_ref[...]) * (1.0 + sc2) + sh2
    h2_ref[...] = h2.astype(BF16)


def _mixer(x, mod, g_norm1, w_in_b, gq, gk, rc, rs1, rs2, conv_w, goa, goc, w_out_b, bd, sinks,
           g_norm2):
    b, s, d = x.shape
    ts = min(MIX_ROWS, s)
    const = lambda shape: pl.BlockSpec(shape, lambda bi, si: (0,) * len(shape),
                                       pipeline_mode=pl.Buffered(1))
    rope_spec = pl.BlockSpec((ts, LANES), lambda bi, si: (si, 0))
    return pl.pallas_call(
        _mixer_kernel,
        grid=(b, s // ts),
        in_specs=[pl.BlockSpec(memory_space=pltpu.SMEM),
                  pl.BlockSpec((None, ts, d), lambda bi, si: (bi, si, 0)),
                  pl.BlockSpec((None, 6, d), lambda bi, si: (bi, 0, 0)),
                  const((1, d)),
                  const((d, IN_COLS)),
                  const((1, ATTN_WIDTH)),
                  const((1, KV_WIDTH)),
                  rope_spec, rope_spec, rope_spec,
                  const((CONV_K, CONV_WIDTH)),
                  const((1, ATTN_WIDTH)),
                  const((1, CONV_WIDTH)),
                  const((d, d)),
                  const((MXU_DIM, MXU_DIM)),
                  const((1, d))],
        out_specs=[pl.BlockSpec((None, ts, d), lambda bi, si: (bi, si, 0)),
                   pl.BlockSpec((None, ts, d), lambda bi, si: (bi, si, 0))],
        out_shape=[jax.ShapeDtypeStruct((b, s, d), F32), jax.ShapeDtypeStruct((b, s, d), BF16)],
        scratch_shapes=[pltpu.VMEM((N_KV_HEADS, 2, ts + WINDOW, LANES), BF16),
                        pltpu.VMEM((N_KV_HEADS, 2, ts + WINDOW, LANES), BF16),
                        pltpu.VMEM((ts + 8, CONV_WIDTH), F32),
                        pltpu.VMEM((ts, ATTN_WIDTH), BF16),
                        pltpu.VMEM((ts, d), BF16)],
        compiler_params=pltpu.CompilerParams(
            dimension_semantics=("arbitrary", "arbitrary"), vmem_limit_bytes=VMEM_LIMIT),
        name="mixer",
    )(sinks, x, mod, g_norm1, w_in_b, gq, gk, rc, rs1, rs2, conv_w, goa, goc, w_out_b, bd, g_norm2)


def _fold_kernel(sk_ref, w_ref, o_ref):
    o_ref[...] = _dot_t(sk_ref[...], w_ref[...]).astype(BF16)


def _fold(subkeys_b, w_pq_b):
    d = w_pq_b.shape[0]
    return pl.pallas_call(
        _fold_kernel,
        grid=(2, PEER_HEADS),
        in_specs=[pl.BlockSpec((None, None, PEER_NKEYS, PEER_HALF), lambda p, h: (h, p, 0, 0)),
                  pl.BlockSpec((d, PEER_HALF), lambda p, h: (0, 2 * h + p))],
        out_specs=pl.BlockSpec((PEER_NKEYS, d), lambda p, h: (p * PEER_HEADS + h, 0)),
        out_shape=jax.ShapeDtypeStruct((2 * PEER_HEADS * PEER_NKEYS, d), BF16),
        compiler_params=pltpu.CompilerParams(
            dimension_semantics=("arbitrary", "arbitrary"), vmem_limit_bytes=VMEM_LIMIT),
        name="fold",
    )(subkeys_b, w_pq_b)


def _compare_exchange(v, i, l, descending):
    hi, lo = jnp.maximum(v[i], v[l]), jnp.minimum(v[i], v[l])
    v[i], v[l] = (hi, lo) if descending else (lo, hi)


def _bitonic_merge(v):
    n = len(v)
    j = n // 2
    while j >= 1:
        for i in range(n):
            if i ^ j > i:
                _compare_exchange(v, i, i ^ j, True)
        j //= 2
    return v


def _top16(s, want_rank=False):
    n = PEER_TOPK
    sub = s.shape[0] // n
    v = [s[sub * j:sub * (j + 1), :] for j in range(n)]
    k = 2
    while k <= n:
        j = k // 2
        while j >= 1:
            for i in range(n):
                if i ^ j > i:
                    _compare_exchange(v, i, i ^ j, (i & k) == 0)
            j //= 2
        k *= 2
    shift = sub // 2
    while shift >= 1:
        rolled = [pltpu.roll(w, shift, 0) for w in v]
        v = _bitonic_merge([jnp.maximum(v[i], rolled[n - 1 - i]) for i in range(n)])
        shift //= 2
    t = [w[0:1, :] for w in v]
    if not want_rank:
        return t
    b8 = t[7] > s
    b4 = jnp.where(b8, t[11], t[3]) > s
    b2 = jnp.where(b8, jnp.where(b4, t[13], t[9]), jnp.where(b4, t[5], t[1])) > s
    b1 = jnp.where(b8,
                   jnp.where(b4, jnp.where(b2, t[14], t[12]), jnp.where(b2, t[10], t[8])),
                   jnp.where(b4, jnp.where(b2, t[6], t[4]), jnp.where(b2, t[2], t[0]))) > s
    rank = (jnp.where(b8, 8.0, 0.0) + jnp.where(b4, 4.0, 0.0) + jnp.where(b2, 2.0, 0.0)
            + jnp.where(b1, 1.0, 0.0) + jnp.where(t[15] > s, 1.0, 0.0))
    return t, rank


def _route_kernel(h2_ref, a_ref, rank2_ref, e2w_ref, cnt_ref, e1w_ref, st0_ref, st1_ref):
    i = pl.program_id(0)

    @pl.when(i == 0)
    def _():
        st1_ref[...] = jnp.zeros(st1_ref.shape, F32)

    hk = PEER_HEADS * PEER_NKEYS
    tr = h2_ref.shape[0]

    def run(st_new, st_old):
        thunks = []
        for g in range(tr // LANES):
            lanes = slice(g * LANES, (g + 1) * LANES)
            t1, t2 = [None] * PEER_HEADS, [None] * PEER_HEADS
            state = {}

            def top1(h, lanes=lanes, t1=t1):
                t1[h] = _top16(st_old[h * PEER_NKEYS:(h + 1) * PEER_NKEYS, lanes])

            def top2(h, g=g, lanes=lanes, t2=t2):
                vals, rank = _top16(st_old[hk + h * PEER_NKEYS:hk + (h + 1) * PEER_NKEYS, lanes], True)
                t2[h] = vals
                rank2_ref[g, h] = rank.astype(BF16)

            def candidates(t1=t1, t2=t2, state=state):
                r1 = [jnp.concatenate([t1[h][a] for h in range(PEER_HEADS)], axis=0)
                      for a in range(PEER_TOPK)]
                r2 = [jnp.concatenate([t2[h][a] for h in range(PEER_HEADS)], axis=0)
                      for a in range(PEER_TOPK)]
                zs = [r1[a] + r2[b] for (a, b) in _PAIRS]
                cur = list(zs)
                z16 = None
                for it in range(PEER_TOPK):
                    z16 = functools.reduce(jnp.maximum, cur)
                    if it + 1 < PEER_TOPK:
                        cur = [jnp.where(c == z16, -jnp.inf, c) for c in cur]
                zmax = zs[0]
                zsum = jnp.zeros_like(zmax)
                cnt = [jnp.zeros_like(zmax) for _ in range(PEER_TOPK)]
                for (a, b), z in zip(_PAIRS, zs):
                    sel = z >= z16
                    zsum = zsum + jnp.where(sel, jnp.exp(z - zmax), 0.0)
                    cnt[a] = cnt[a] + jnp.where(sel, 1.0, 0.0)
                state["cnt"] = cnt
                state["inv_z"] = 1.0 / zsum

            def finish(h, g=g, lanes=lanes, t1=t1, t2=t2, state=state):
                s1 = st_old[h * PEER_NKEYS:(h + 1) * PEER_NKEYS, lanes]
                s2 = st_old[hk + h * PEER_NKEYS:hk + (h + 1) * PEER_NKEYS, lanes]
                cnt_full = jnp.zeros_like(s1)
                for a in range(PEER_TOPK):
                    cnt_full = jnp.where(s1 == t1[h][a], state["cnt"][a][h:h + 1, :], cnt_full)
                e2w_ref[g, h] = (jnp.exp(s2 - t2[h][0]) * state["inv_z"][h:h + 1, :]).astype(BF16)
                cnt_ref[g, h] = cnt_full
                e1w_ref[g, h] = jnp.exp(s1 - t1[h][0])

            for h in range(PEER_HEADS):
                thunks.append(functools.partial(top1, h))
                thunks.append(functools.partial(top2, h))
            thunks.append(candidates)
            for h in range(PEER_HEADS):
                thunks.append(functools.partial(finish, h))

        rows_per = 512
        n_rb, n_k = st_new.shape[0] // rows_per, h2_ref.shape[1] // MXU_DIM
        n_dots, done_dots, done_thunks = n_rb * n_k, 0, 0
        for rb in range(n_rb):
            rows = slice(rb * rows_per, (rb + 1) * rows_per)
            acc = None
            for k in range(n_k):
                ks = slice(k * MXU_DIM, (k + 1) * MXU_DIM)
                part = _dot_t(a_ref[rows, ks], h2_ref[:, ks])
                acc = part if acc is None else acc + part
                done_dots += 1
                target = -(-len(thunks) * done_dots // n_dots)
                while done_thunks < target:
                    thunks[done_thunks]()
                    done_thunks += 1
            st_new[rows, :] = acc

    @pl.when(i % 2 == 0)
    def _():
        run(st0_ref, st1_ref)

    @pl.when(i % 2 == 1)
    def _():
        run(st1_ref, st0_ref)


def _route(h2, a_mat, seq):
    n, d = h2.shape
    tr = min(ROUTE_ROWS, seq)
    grp = tr // LANES
    tiles = n // tr
    cur = lambda i: jnp.minimum(i, tiles - 1)
    prev = lambda i: jnp.maximum(i - 1, 0)
    rshape = (n // LANES, PEER_HEADS, PEER_NKEYS, LANES)
    rspec = pl.BlockSpec((grp, PEER_HEADS, PEER_NKEYS, LANES), lambda i: (prev(i), 0, 0, 0))
    return pl.pallas_call(
        _route_kernel,
        grid=(tiles + 1,),
        in_specs=[pl.BlockSpec((tr, d), lambda i: (cur(i), 0)),
                  pl.BlockSpec(a_mat.shape, lambda i: (0, 0), pipeline_mode=pl.Buffered(1))],
        out_specs=[rspec, rspec, rspec, rspec],
        out_shape=[jax.ShapeDtypeStruct(rshape, BF16), jax.ShapeDtypeStruct(rshape, BF16),
                   jax.ShapeDtypeStruct(rshape, F32), jax.ShapeDtypeStruct(rshape, F32)],
        scratch_shapes=[pltpu.VMEM((a_mat.shape[0], tr), F32), pltpu.VMEM((a_mat.shape[0], tr), F32)],
        compiler_params=pltpu.CompilerParams(
            dimension_semantics=("arbitrary",), vmem_limit_bytes=VMEM_LIMIT),
        name="route",
    )(h2, a_mat)


def _peer_kernel(h2_ref, u_ref, vt_ref, rank2_ref, e2w_ref, cnt_ref, e1w_ref, yt_ref,
                 a0_scr, a1_scr, act0_scr, act1_scr):
    g = pl.program_id(0)
    groups, ec = a0_scr.shape[0], a0_scr.shape[1]
    d = yt_ref.shape[0]
    n_chunks = PEER_NKEYS // PEER_E1

    @pl.when(g == 0)
    def _():
        for ref in (a0_scr, a1_scr, act0_scr, act1_scr):
            ref[...] = jnp.zeros(ref.shape, ref.dtype)

    @pl.when((g == 0) | ((g >= 2) & ((g - 2) % n_chunks == 0)))
    def _():
        yt_ref[...] = jnp.zeros(yt_ref.shape, F32)

    sqrt_half = math.sqrt(0.5)

    def run(a_new, a_old, act_new, act_old):
        nblk = PEER_NKEYS // E2_ROWS

        def gate_thunks(e1l, tb):
            w = [jnp.zeros((E2_ROWS, LANES), BF16) for _ in range(nblk)]

            def head_part(h):
                cnt = jnp.broadcast_to(cnt_ref[tb, h, e1l:e1l + 1, :], (E2_ROWS, LANES)).astype(BF16)
                w1 = jnp.broadcast_to(e1w_ref[tb, h, e1l:e1l + 1, :], (E2_ROWS, LANES)).astype(BF16)
                for eb in range(nblk):
                    e2rows = slice(eb * E2_ROWS, (eb + 1) * E2_ROWS)
                    w[eb] = w[eb] + w1 * jnp.where(rank2_ref[tb, h, e2rows, :] < cnt,
                                                   e2w_ref[tb, h, e2rows, :], jnp.zeros((), BF16))

            def finish_part(eb):
                rows = slice(e1l * PEER_NKEYS + eb * E2_ROWS,
                             e1l * PEER_NKEYS + (eb + 1) * E2_ROWS)
                a = a_old[tb, rows, :]
                gelu = 0.5 * a * (1.0 + lax.erf(a * sqrt_half))
                lanes = slice((tb % 2) * LANES, (tb % 2 + 1) * LANES)
                act_new[tb // 2, rows, lanes] = gelu.astype(BF16) * w[eb]

            return ([functools.partial(head_part, h) for h in range(PEER_HEADS)]
                    + [functools.partial(finish_part, eb) for eb in range(nblk)])

        thunks = [t for e1l in range(PEER_E1) for tb in range(groups) for t in gate_thunks(e1l, tb)]

        halves = groups // 2
        kt_score, kt_value = d // MXU_DIM, ec // MXU_DIM
        score_rows, value_rows = min(SCORE_ROWS, ec), min(VALUE_ROWS, d)
        mxu_ops = [("score", rb, nb) for rb in range(ec // score_rows) for nb in range(halves)]
        mxu_ops += [("value", rb, nb) for rb in range(d // value_rows) for nb in range(halves)]
        n_small = halves * ((ec // MXU_DIM) * kt_score + (d // MXU_DIM) * kt_value)
        done_small, done_thunks = 0, 0

        def vpu_fill():
            nonlocal done_thunks
            target = -(-len(thunks) * done_small // n_small)
            while done_thunks < min(target, len(thunks)):
                thunks[done_thunks]()
                done_thunks += 1

        for kind, rb, nb in mxu_ops:
            tok = slice(nb * MXU_DIM, (nb + 1) * MXU_DIM)
            acc = None
            if kind == "score":
                rows = slice(rb * score_rows, (rb + 1) * score_rows)
                for k in range(kt_score):
                    ks = slice(k * MXU_DIM, (k + 1) * MXU_DIM)
                    part = _dot_t(u_ref[rows, ks], h2_ref[tok, ks])
                    acc = part if acc is None else acc + part
                    done_small += score_rows // MXU_DIM
                    vpu_fill()
                a_new[2 * nb, rows, :] = acc[:, :LANES]
                a_new[2 * nb + 1, rows, :] = acc[:, LANES:]
            else:
                rows = slice(rb * value_rows, (rb + 1) * value_rows)
                for k in range(kt_value):
                    ks = slice(k * MXU_DIM, (k + 1) * MXU_DIM)
                    part = jnp.dot(vt_ref[rows, ks], act_old[nb, ks, :], preferred_element_type=F32)
                    acc = part if acc is None else acc + part
                    done_small += value_rows // MXU_DIM
                    vpu_fill()
                yt_ref[rows, tok] += acc
        assert done_thunks == len(thunks)

    @pl.when(g % 2 == 0)
    def _():
        run(a0_scr, a1_scr, act1_scr, act0_scr)

    @pl.when(g % 2 == 1)
    def _():
        run(a1_scr, a0_scr, act0_scr, act1_scr)


def _peer(h2, u_b, vt_b, rank2, e2w, cnt, e1w):
    n, d = h2.shape
    tp = min(PEER_ROWS, n)
    grp = tp // LANES
    ec = PEER_E1 * PEER_NKEYS
    nc = PEER_N // ec
    total = (n // tp) * nc
    lag = lambda g, k: jnp.clip(g - k, 0, total - 1)
    full = pl.BlockSpec((grp, PEER_HEADS, PEER_NKEYS, LANES), lambda g: (lag(g, 1) // nc, 0, 0, 0))
    part = pl.BlockSpec((grp, PEER_HEADS, PEER_E1, LANES),
                        lambda g: (lag(g, 1) // nc, 0, lag(g, 1) % nc, 0))
    return pl.pallas_call(
        _peer_kernel,
        grid=(total + 2,),
        in_specs=[pl.BlockSpec((tp, d), lambda g: (lag(g, 0) // nc, 0)),
                  pl.BlockSpec((ec, d), lambda g: (lag(g, 0) % nc, 0)),
                  pl.BlockSpec((None, d, ec), lambda g: (lag(g, 2) % nc, 0, 0)),
                  full, full, part, part],
        out_specs=pl.BlockSpec((d, tp), lambda g: (0, lag(g, 2) // nc)),
        out_shape=jax.ShapeDtypeStruct((d, n), F32),
        scratch_shapes=[pltpu.VMEM((grp, ec, LANES), F32), pltpu.VMEM((grp, ec, LANES), F32),
                        pltpu.VMEM((grp // 2, ec, MXU_DIM), BF16),
                        pltpu.VMEM((grp // 2, ec, MXU_DIM), BF16)],
        compiler_params=pltpu.CompilerParams(
            dimension_semantics=("arbitrary",), vmem_limit_bytes=VMEM_LIMIT),
        name="peer",
    )(h2, u_b, vt_b, rank2, e2w, cnt, e1w)


def _final_kernel(x_ref, mod_ref, yt_ref, o_ref):
    gt2 = mod_ref[...][5:6]
    o_ref[...] = x_ref[...] + gt2 * yt_ref[...].T


def _final(x1, mod, yt):
    b, s, d = x1.shape
    tf = min(PEER_ROWS, s)
    per_seq = s // tf
    return pl.pallas_call(
        _final_kernel,
        grid=(b, per_seq),
        in_specs=[pl.BlockSpec((None, tf, d), lambda bi, si: (bi, si, 0)),
                  pl.BlockSpec((None, 6, d), lambda bi, si: (bi, 0, 0)),
                  pl.BlockSpec((d, tf), lambda bi, si: (0, bi * per_seq + si))],
        out_specs=pl.BlockSpec((None, tf, d), lambda bi, si: (bi, si, 0)),
        out_shape=jax.ShapeDtypeStruct((b, s, d), F32),
        compiler_params=pltpu.CompilerParams(
            dimension_semantics=("arbitrary", "arbitrary"), vmem_limit_bytes=VMEM_LIMIT),
        name="final",
    )(x1, mod, yt)


def _rope_tables(s):
    pos = jnp.arange(s, dtype=F32)
    inv_freq = ROPE_THETA ** (-jnp.arange(0, ROPE_DIM, 2, dtype=F32) / ROPE_DIM)
    ang = pos[:, None] * inv_freq[None, :]
    cos, sin = jnp.cos(ang), jnp.sin(ang)
    half = ROPE_DIM // 2
    ones = jnp.ones((s, HEAD_DIM - ROPE_DIM), F32)
    zeros = jnp.zeros((s, HEAD_DIM - ROPE_DIM), F32)
    zh = jnp.zeros((s, half), F32)
    rc = jnp.concatenate([cos, cos, ones], axis=1)
    rs1 = jnp.concatenate([-sin, zh, zeros], axis=1)
    rs2 = jnp.concatenate([zh, sin, zeros], axis=1)
    tile = lambda t: jnp.tile(t, (1, LANES // HEAD_DIM))
    return tile(rc), tile(rs1), tile(rs2)


def kernel(x, c, w_ada, b_ada, g_norm1, w_in, g_q, g_k, sinks, conv_w, g_out_attn,
           g_out_conv, w_out, g_norm2, w_pq, peer_subkeys, peer_u, peer_v):
    b, s, d = x.shape
    assert d == D_MODEL and s % WINDOW == 0 and (b * s) % LANES == 0

    mod = _ada(c, w_ada, b_ada).reshape(b, 6, d)

    rc, rs1, rs2 = _rope_tables(s)
    blk = jnp.arange(MXU_DIM) // HEAD_DIM
    bd = (blk[:, None] == blk[None, :]).astype(BF16)
    x1, h2 = _mixer(x, mod, g_norm1.reshape(1, d), w_in.astype(BF16),
                jnp.tile(g_q, N_Q_HEADS).reshape(1, ATTN_WIDTH),
                jnp.tile(g_k, N_KV_HEADS).reshape(1, KV_WIDTH),
                rc, rs1, rs2, conv_w, g_out_attn.reshape(1, ATTN_WIDTH),
                g_out_conv.reshape(1, CONV_WIDTH), w_out.astype(BF16), bd, sinks,
                g_norm2.reshape(1, d))

    a_mat = _fold(peer_subkeys.astype(BF16), w_pq.astype(BF16))
    h2 = h2.reshape(b * s, d)
    rank2, e2w, cnt, e1w = _route(h2, a_mat, s)
    ec = PEER_E1 * PEER_NKEYS
    vt = peer_v.astype(BF16).reshape(PEER_N // ec, ec, d).transpose(0, 2, 1)
    yt = _peer(h2, peer_u.astype(BF16), vt, rank2, e2w, cnt, e1w)
    return _final(x1, mod, yt)
```

```python
import functools
import math

import jax
import jax.numpy as jnp
from jax import lax
from jax.experimental import pallas as pl
from jax.experimental.pallas import tpu as pltpu

F32 = jnp.float32
BF16 = jnp.bfloat16

D_MODEL = 2048
HEAD_DIM = 64
N_Q_HEADS = 16
N_KV_HEADS = 4
Q_PER_KV = N_Q_HEADS // N_KV_HEADS
ATTN_WIDTH = N_Q_HEADS * HEAD_DIM
KV_WIDTH = N_KV_HEADS * HEAD_DIM
CONV_WIDTH = D_MODEL - ATTN_WIDTH
CONV_K = 3
IN_COLS = ATTN_WIDTH + 2 * KV_WIDTH + 3 * CONV_WIDTH
WINDOW = 128
ROPE_THETA = 500000.0
ROPE_DIM = HEAD_DIM // 4
ATTN_SCALE = HEAD_DIM ** -0.5
NEG_INF = -1e30
PEER_HEADS = 8
PEER_NKEYS = 128
PEER_N = PEER_NKEYS * PEER_NKEYS
PEER_HALF = 128
PEER_TOPK = 16
EPS = 1e-6

LANES = 128
MXU_DIM = 256
VMEM_LIMIT = 56 * 1024 * 1024

ADA_COLS = 1024
MIX_ROWS = 256
ROUTE_ROWS = 256
PEER_ROWS = 512
PEER_E1 = 8
E2_ROWS = 32
SCORE_ROWS = 512
VALUE_ROWS = 512

_PAIRS = tuple((a, b) for a in range(PEER_TOPK) for b in range(PEER_TOPK)
               if (a + 1) * (b + 1) <= PEER_TOPK)


def _dot_t(a, b):
    return lax.dot_general(a, b, (((1,), (1,)), ((), ())),
                           preferred_element_type=F32)


def _group_mean_sq(y, bd):
    y2 = y * y
    hi = y2.astype(BF16)
    lo = (y2 - hi.astype(F32)).astype(BF16)
    outs = []
    for j in range(y.shape[1] // MXU_DIM):
        sl = slice(MXU_DIM * j, MXU_DIM * (j + 1))
        outs.append(jnp.dot(hi[:, sl], bd, preferred_element_type=F32)
                    + jnp.dot(lo[:, sl], bd, preferred_element_type=F32))
    ss = outs[0] if len(outs) == 1 else jnp.concatenate(outs, axis=1)
    return ss * (1.0 / HEAD_DIM)


def _rope(t, rc, rs1, rs2):
    outs = []
    for j in range(t.shape[1] // LANES):
        tj = t[:, LANES * j:LANES * (j + 1)]
        outs.append(tj * rc
                    + pltpu.roll(tj, LANES - ROPE_DIM // 2, 1) * rs1
                    + pltpu.roll(tj, ROPE_DIM // 2, 1) * rs2)
    return outs[0] if len(outs) == 1 else jnp.concatenate(outs, axis=1)


def _ada_kernel(c_ref, w_ref, b_ref, o_ref):
    c = c_ref[...]
    s = c * jax.nn.sigmoid(c)
    o_ref[...] = jnp.dot(s, w_ref[...], preferred_element_type=F32,
                         precision=lax.Precision.HIGHEST) + b_ref[...]


def _ada(c, w_ada, b_ada):
    b, d = c.shape
    n = w_ada.shape[1]
    return pl.pallas_call(
        _ada_kernel,
        grid=(n // ADA_COLS,),
        in_specs=[pl.BlockSpec((b, d), lambda j: (0, 0)),
                  pl.BlockSpec((d, ADA_COLS), lambda j: (0, j)),
                  pl.BlockSpec((1, ADA_COLS), lambda j: (0, j))],
        out_specs=pl.BlockSpec((b, ADA_COLS), lambda j: (0, j)),
        out_shape=jax.ShapeDtypeStruct((b, n), F32),
        compiler_params=pltpu.CompilerParams(
            dimension_semantics=("arbitrary",), vmem_limit_bytes=VMEM_LIMIT),
        name="ada",
    )(c, w_ada, b_ada.reshape(1, n))


def _mixer_kernel(sinks_ref, x_ref, mod_ref, g1_ref, win_ref, gq_ref, gk_ref,
                  rc_ref, rs1_ref, rs2_ref, cw_ref, goa_ref, goc_ref, wout_ref,
                  bd_ref, g2_ref, o_ref, h2_ref, kk_ref, vv_ref, u_ref, qn_ref, mixed_ref):
    ts = x_ref.shape[0]
    si = pl.program_id(1)
    x = x_ref[...]
    mod = mod_ref[...]
    sh1, sc1, gt1 = mod[0:1], mod[1:2], mod[2:3]
    bd = bd_ref[...]
    rc, rs1, rs2 = rc_ref[...], rs1_ref[...], rs2_ref[...]

    @pl.when(si == 0)
    def _():
        kk_ref[:, :, 0:WINDOW, :] = jnp.zeros((N_KV_HEADS, 2, WINDOW, LANES), BF16)
        vv_ref[:, :, 0:WINDOW, :] = jnp.zeros((N_KV_HEADS, 2, WINDOW, LANES), BF16)
        u_ref[0:8, :] = jnp.zeros((8, CONV_WIDTH), F32)

    @pl.when(si > 0)
    def _():
        kk_ref[:, :, 0:WINDOW, :] = kk_ref[:, :, ts:ts + WINDOW, :]
        vv_ref[:, :, 0:WINDOW, :] = vv_ref[:, :, ts:ts + WINDOW, :]
        u_ref[0:8, :] = u_ref[ts:ts + 8, :]

    ms = jnp.mean(x * x, axis=-1, keepdims=True)
    h1 = ((x * lax.rsqrt(ms + EPS)) * g1_ref[...]) * (1.0 + sc1) + sh1
    h1b = h1.astype(BF16)

    def proj(lo, width):
        return jnp.dot(h1b, win_ref[:, lo:lo + width], preferred_element_type=F32)

    q = proj(0, ATTN_WIDTH)
    q = q * lax.rsqrt(_group_mean_sq(q, bd) + EPS) * gq_ref[...]
    qn_ref[...] = (_rope(q, rc, rs1, rs2) * ATTN_SCALE).astype(BF16)

    k = proj(ATTN_WIDTH, KV_WIDTH)
    k = k * lax.rsqrt(_group_mean_sq(k, bd) + EPS) * gk_ref[...]
    k = _rope(k, rc, rs1, rs2)
    v = proj(ATTN_WIDTH + KV_WIDTH, KV_WIDTH)
    low = lax.broadcasted_iota(jnp.int32, (1, LANES), 1) < HEAD_DIM
    for g in range(N_KV_HEADS):
        col = slice(LANES * (g // 2), LANES * (g // 2 + 1))
        for t, dst in ((k[:, col], kk_ref), (v[:, col], vv_ref)):
            tr = pltpu.roll(t, HEAD_DIM, 1)
            in_low, in_high = (t, tr) if g % 2 == 0 else (tr, t)
            dst[g, 0, WINDOW:WINDOW + ts, :] = jnp.where(low, in_low, 0.0).astype(BF16)
            dst[g, 1, WINDOW:WINDOW + ts, :] = jnp.where(low, 0.0, in_high).astype(BF16)

    qi = lax.broadcasted_iota(jnp.int32, (WINDOW, 2 * WINDOW), 0)
    kj = lax.broadcasted_iota(jnp.int32, (WINDOW, 2 * WINDOW), 1)
    band = (kj > qi) & (kj <= qi + WINDOW)

    def scores(jb, g):
        qrows = slice(jb * WINDOW, (jb + 1) * WINDOW)
        krows = slice(jb * WINDOW, (jb + 2) * WINDOW)
        qa = qn_ref[qrows, MXU_DIM * g:MXU_DIM * g + LANES]
        qb = qn_ref[qrows, MXU_DIM * g + LANES:MXU_DIM * (g + 1)]
        q2 = jnp.concatenate([qa, qb], axis=0)
        return (_dot_t(q2, kk_ref[g, 0, krows, :]),
                _dot_t(q2, kk_ref[g, 1, krows, :]))

    def attend(jb, g, s_lo, s_hi):
        first_key = jnp.where(si * (ts // WINDOW) + jb == 0, WINDOW, 0)
        mask = band & (kj >= first_key)
        qrows = slice(jb * WINDOW, (jb + 1) * WINDOW)
        krows = slice(jb * WINDOW, (jb + 2) * WINDOW)
        vcat = jnp.concatenate([vv_ref[g, 0, krows, :], vv_ref[g, 1, krows, :]], axis=0)
        halves = []
        for half in range(2):
            rows = slice(half * WINDOW, (half + 1) * WINDOW)
            ps, inv = [], []
            for o, s in enumerate((s_lo[rows], s_hi[rows])):
                sink = sinks_ref[Q_PER_KV * g + 2 * half + o]
                s = jnp.where(mask, s, NEG_INF)
                m = jnp.maximum(jnp.max(s, axis=-1, keepdims=True), sink)
                p = jnp.exp(s - m)
                denom = jnp.sum(p, axis=-1, keepdims=True) + jnp.exp(sink - m)
                ps.append(p.astype(BF16))
                inv.append(1.0 / denom)
            pv = jnp.dot(jnp.concatenate(ps, axis=1), vcat, preferred_element_type=F32)
            halves.append(pv * jnp.where(low, inv[0], inv[1]))
        att = jnp.concatenate(halves, axis=1)
        cols = slice(MXU_DIM * g, MXU_DIM * (g + 1))
        att = att * lax.rsqrt(_group_mean_sq(att, bd) + EPS) * goa_ref[:, cols]
        mixed_ref[qrows, cols] = att.astype(BF16)

    o0 = ATTN_WIDTH + 2 * KV_WIDTH
    cw = cw_ref[...]

    def conv_block(cb):
        cols = slice(cb * MXU_DIM, (cb + 1) * MXU_DIM)
        bg = proj(o0 + cb * MXU_DIM, MXU_DIM)
        u_ref[8:8 + ts, cols] = (proj(o0 + CONV_WIDTH + cb * MXU_DIM, MXU_DIM)
                                 * proj(o0 + 2 * CONV_WIDTH + cb * MXU_DIM, MXU_DIM))
        acc = cw[0:1, cols] * u_ref[6:6 + ts, cols]
        acc = acc + cw[1:2, cols] * u_ref[7:7 + ts, cols]
        acc = acc + cw[2:3, cols] * u_ref[8:8 + ts, cols]
        y = bg * acc
        y = y * lax.rsqrt(_group_mean_sq(y, bd) + EPS) * goc_ref[:, cols]
        mixed_ref[:, ATTN_WIDTH + cb * MXU_DIM:ATTN_WIDTH + (cb + 1) * MXU_DIM] = y.astype(BF16)

    iters = [(jb, g) for jb in range(ts // WINDOW) for g in range(N_KV_HEADS)]
    n_conv = CONV_WIDTH // MXU_DIM
    conv_done = 0
    s_cur = scores(*iters[0])
    for i, (jb, g) in enumerate(iters):
        s_next = scores(*iters[i + 1]) if i + 1 < len(iters) else None
        want = -(-n_conv * (i + 1) // len(iters))
        while conv_done < want:
            conv_block(conv_done)
            conv_done += 1
        attend(jb, g, *s_cur)
        s_cur = s_next

    x1 = x + gt1 * jnp.dot(mixed_ref[...], wout_ref[...], preferred_element_type=F32)
    o_ref[...] = x1

    sh2, sc2 = mod[3:4], mod[4:5]
    ms2 = jnp.mean(x1 * x1, axis=-1, keepdims=True)
    h2 = ((x1 * lax.rsqrt(ms2 + EPS)) * g2_ref[...]) * (1.0 + sc2) + sh2
    h2_ref[...] = h2.astype(BF16)


def _mixer(x, mod, g_norm1, w_in_b, gq, gk, rc, rs1, rs2, conv_w, goa, goc, w_out_b, bd, sinks,
           g_norm2):
    b, s, d = x.shape
    ts = min(MIX_ROWS, s)
    const = lambda shape: pl.BlockSpec(shape, lambda bi, si: (0,) * len(shape),
                                       pipeline_mode=pl.Buffered(1))
    rope_spec = pl.BlockSpec((ts, LANES), lambda bi, si: (si, 0))
    return pl.pallas_call(
        _mixer_kernel,
        grid=(b, s // ts),
        in_specs=[pl.BlockSpec(memory_space=pltpu.SMEM),
                  pl.BlockSpec((None, ts, d), lambda bi, si: (bi, si, 0)),
                  pl.BlockSpec((None, 6, d), lambda bi, si: (bi, 0, 0)),
                  const((1, d)),
                  const((d, IN_COLS)),
                  const((1, ATTN_WIDTH)),
                  const((1, KV_WIDTH)),
                  rope_spec, rope_spec, rope_spec,
                  const((CONV_K, CONV_WIDTH)),
                  const((1, ATTN_WIDTH)),
                  const((1, CONV_WIDTH)),
                  const((d, d)),
                  const((MXU_DIM, MXU_DIM)),
                  const((1, d))],
        out_specs=[pl.BlockSpec((None, ts, d), lambda bi, si: (bi, si, 0)),
                   pl.BlockSpec((None, ts, d), lambda bi, si: (bi, si, 0))],
        out_shape=[jax.ShapeDtypeStruct((b, s, d), F32), jax.ShapeDtypeStruct((b, s, d), BF16)],
        scratch_shapes=[pltpu.VMEM((N_KV_HEADS, 2, ts + WINDOW, LANES), BF16),
                        pltpu.VMEM((N_KV_HEADS, 2, ts + WINDOW, LANES), BF16),
                        pltpu.VMEM((ts + 8, CONV_WIDTH), F32),
                        pltpu.VMEM((ts, ATTN_WIDTH), BF16),
                        pltpu.VMEM((ts, d), BF16)],
        compiler_params=pltpu.CompilerParams(
            dimension_semantics=("arbitrary", "arbitrary"), vmem_limit_bytes=VMEM_LIMIT),
        name="mixer",
    )(sinks, x, mod, g_norm1, w_in_b, gq, gk, rc, rs1, rs2, conv_w, goa, goc, w_out_b, bd, g_norm2)


def _fold_kernel(sk_ref, w_ref, o_ref):
    o_ref[...] = _dot_t(sk_ref[...], w_ref[...]).astype(BF16)


def _fold(subkeys_b, w_pq_b):
    d = w_pq_b.shape[0]
    return pl.pallas_call(
        _fold_kernel,
        grid=(2, PEER_HEADS),
        in_specs=[pl.BlockSpec((None, None, PEER_NKEYS, PEER_HALF), lambda p, h: (h, p, 0, 0)),
                  pl.BlockSpec((d, PEER_HALF), lambda p, h: (0, 2 * h + p))],
        out_specs=pl.BlockSpec((PEER_NKEYS, d), lambda p, h: (p * PEER_HEADS + h, 0)),
        out_shape=jax.ShapeDtypeStruct((2 * PEER_HEADS * PEER_NKEYS, d), BF16),
        compiler_params=pltpu.CompilerParams(
            dimension_semantics=("arbitrary", "arbitrary"), vmem_limit_bytes=VMEM_LIMIT),
        name="fold",
    )(subkeys_b, w_pq_b)


def _compare_exchange(v, i, l, descending):
    hi, lo = jnp.maximum(v[i], v[l]), jnp.minimum(v[i], v[l])
    v[i], v[l] = (hi, lo) if descending else (lo, hi)


def _bitonic_merge(v):
    n = len(v)
    j = n // 2
    while j >= 1:
        for i in range(n):
            if i ^ j > i:
                _compare_exchange(v, i, i ^ j, True)
        j //= 2
    return v


def _top16(s, want_rank=False):
    n = PEER_TOPK
    sub = s.shape[0] // n
    v = [s[sub * j:sub * (j + 1), :] for j in range(n)]
    k = 2
    while k <= n:
        j = k // 2
        while j >= 1:
            for i in range(n):
                if i ^ j > i:
                    _compare_exchange(v, i, i ^ j, (i & k) == 0)
            j //= 2
        k *= 2
    shift = sub // 2
    while shift >= 1:
        rolled = [pltpu.roll(w, shift, 0) for w in v]
        v = _bitonic_merge([jnp.maximum(v[i], rolled[n - 1 - i]) for i in range(n)])
        shift //= 2
    t = [w[0:1, :] for w in v]
    if not want_rank:
        return t
    b8 = t[7] > s
    b4 = jnp.where(b8, t[11], t[3]) > s
    b2 = jnp.where(b8, jnp.where(b4, t[13], t[9]), jnp.where(b4, t[5], t[1])) > s
    b1 = jnp.where(b8,
                   jnp.where(b4, jnp.where(b2, t[14], t[12]), jnp.where(b2, t[10], t[8])),
                   jnp.where(b4, jnp.where(b2, t[6], t[4]), jnp.where(b2, t[2], t[0]))) > s
    rank = (jnp.where(b8, 8.0, 0.0) + jnp.where(b4, 4.0, 0.0) + jnp.where(b2, 2.0, 0.0)
            + jnp.where(b1, 1.0, 0.0) + jnp.where(t[15] > s, 1.0, 0.0))
    return t, rank


def _route_kernel(h2_ref, a_ref, rank2_ref, e2w_ref, ce_ref, st0_ref, st1_ref):
    i = pl.program_id(0)

    @pl.when(i == 0)
    def _():
        st1_ref[...] = jnp.zeros(st1_ref.shape, F32)

    hk = PEER_HEADS * PEER_NKEYS
    tr = h2_ref.shape[0]

    def run(st_new, st_old):
        thunks = []
        for g in range(tr // LANES):
            lanes = slice(g * LANES, (g + 1) * LANES)
            t1, t2 = [None] * PEER_HEADS, [None] * PEER_HEADS
            state = {}

            def top1(h, lanes=lanes, t1=t1):
                t1[h] = _top16(st_old[h * PEER_NKEYS:(h + 1) * PEER_NKEYS, lanes])

            def top2(h, g=g, lanes=lanes, t2=t2):
                vals, rank = _top16(st_old[hk + h * PEER_NKEYS:hk + (h + 1) * PEER_NKEYS, lanes], True)
                t2[h] = vals
                rank2_ref[g, h] = rank.astype(BF16)

            def candidates(t1=t1, t2=t2, state=state):
                r1 = [jnp.concatenate([t1[h][a] for h in range(PEER_HEADS)], axis=0)
                      for a in range(PEER_TOPK)]
                r2 = [jnp.concatenate([t2[h][a] for h in range(PEER_HEADS)], axis=0)
                      for a in range(PEER_TOPK)]
                zs = [r1[a] + r2[b] for (a, b) in _PAIRS]
                cur = list(zs)
                z16 = None
                for it in range(PEER_TOPK):
                    z16 = functools.reduce(jnp.maximum, cur)
                    if it + 1 < PEER_TOPK:
                        cur = [jnp.where(c == z16, -jnp.inf, c) for c in cur]
                zmax = zs[0]
                zsum = jnp.zeros_like(zmax)
                cnt = [jnp.zeros_like(zmax) for _ in range(PEER_TOPK)]
                for (a, b), z in zip(_PAIRS, zs):
                    sel = z >= z16
                    zsum = zsum + jnp.where(sel, jnp.exp(z - zmax), 0.0)
                    cnt[a] = cnt[a] + jnp.where(sel, 1.0, 0.0)
                state["cnt"] = cnt
                state["inv_z"] = 1.0 / zsum

            def finish(h, g=g, lanes=lanes, t1=t1, t2=t2, state=state):
                s1 = st_old[h * PEER_NKEYS:(h + 1) * PEER_NKEYS, lanes]
                s2 = st_old[hk + h * PEER_NKEYS:hk + (h + 1) * PEER_NKEYS, lanes]
                cnt_full = jnp.zeros_like(s1)
                for a in range(PEER_TOPK):
                    cnt_full = jnp.where(s1 == t1[h][a], state["cnt"][a][h:h + 1, :], cnt_full)
                e2w_ref[g, h] = (jnp.exp(s2 - t2[h][0]) * state["inv_z"][h:h + 1, :]).astype(BF16)
                ce_ref[g, 0, h] = cnt_full
                ce_ref[g, 1, h] = jnp.exp(s1 - t1[h][0])

            for h in range(PEER_HEADS):
                thunks.append(functools.partial(top1, h))
                thunks.append(functools.partial(top2, h))
            thunks.append(candidates)
            for h in range(PEER_HEADS):
                thunks.append(functools.partial(finish, h))

        rows_per = 512
        n_rb, n_k = st_new.shape[0] // rows_per, h2_ref.shape[1] // MXU_DIM
        n_dots, done_dots, done_thunks = n_rb * n_k, 0, 0
        for rb in range(n_rb):
            rows = slice(rb * rows_per, (rb + 1) * rows_per)
            acc = None
            for k in range(n_k):
                ks = slice(k * MXU_DIM, (k + 1) * MXU_DIM)
                part = _dot_t(a_ref[rows, ks], h2_ref[:, ks])
                acc = part if acc is None else acc + part
                done_dots += 1
                target = -(-len(thunks) * done_dots // n_dots)
                while done_thunks < target:
                    thunks[done_thunks]()
                    done_thunks += 1
            st_new[rows, :] = acc

    @pl.when(i % 2 == 0)
    def _():
        run(st0_ref, st1_ref)

    @pl.when(i % 2 == 1)
    def _():
        run(st1_ref, st0_ref)


def _route(h2, a_mat, seq):
    n, d = h2.shape
    tr = min(ROUTE_ROWS, seq)
    grp = tr // LANES
    tiles = n // tr
    cur = lambda i: jnp.minimum(i, tiles - 1)
    prev = lambda i: jnp.maximum(i - 1, 0)
    rshape = (n // LANES, PEER_HEADS, PEER_NKEYS, LANES)
    rspec = pl.BlockSpec((grp, PEER_HEADS, PEER_NKEYS, LANES), lambda i: (prev(i), 0, 0, 0))
    return pl.pallas_call(
        _route_kernel,
        grid=(tiles + 1,),
        in_specs=[pl.BlockSpec((tr, d), lambda i: (cur(i), 0)),
                  pl.BlockSpec(a_mat.shape, lambda i: (0, 0), pipeline_mode=pl.Buffered(1))],
        out_specs=[rspec, rspec,
                   pl.BlockSpec((grp, 2, PEER_HEADS, PEER_NKEYS, LANES), lambda i: (prev(i), 0, 0, 0, 0))],
        out_shape=[jax.ShapeDtypeStruct(rshape, BF16), jax.ShapeDtypeStruct(rshape, BF16),
                   jax.ShapeDtypeStruct((n // LANES, 2, PEER_HEADS, PEER_NKEYS, LANES), F32)],
        scratch_shapes=[pltpu.VMEM((a_mat.shape[0], tr), F32), pltpu.VMEM((a_mat.shape[0], tr), F32)],
        compiler_params=pltpu.CompilerParams(
            dimension_semantics=("arbitrary",), vmem_limit_bytes=VMEM_LIMIT),
        name="route",
    )(h2, a_mat)


def _peer_kernel(h2_ref, u_ref, vt_ref, rank2_ref, e2w_ref, ce_ref, yt_ref,
                 a0_scr, a1_scr, act0_scr, act1_scr):
    g = pl.program_id(0)
    groups, ec = a0_scr.shape[0], a0_scr.shape[1]
    d = yt_ref.shape[0]
    n_chunks = PEER_NKEYS // PEER_E1

    @pl.when(g == 0)
    def _():
        for ref in (a0_scr, a1_scr, act0_scr, act1_scr):
            ref[...] = jnp.zeros(ref.shape, ref.dtype)

    @pl.when((g == 0) | ((g >= 2) & ((g - 2) % n_chunks == 0)))
    def _():
        yt_ref[...] = jnp.zeros(yt_ref.shape, F32)

    sqrt_half = math.sqrt(0.5)

    def run(a_new, a_old, act_new, act_old):
        nblk = PEER_NKEYS // E2_ROWS

        def gate_thunks(e1l, tb):
            w = [jnp.zeros((E2_ROWS, LANES), BF16) for _ in range(nblk)]

            def head_part(h):
                cnt = jnp.broadcast_to(ce_ref[tb, 0, h, e1l:e1l + 1, :], (E2_ROWS, LANES)).astype(BF16)
                w1 = jnp.broadcast_to(ce_ref[tb, 1, h, e1l:e1l + 1, :], (E2_ROWS, LANES)).astype(BF16)
                for eb in range(nblk):
                    e2rows = slice(eb * E2_ROWS, (eb + 1) * E2_ROWS)
                    w[eb] = w[eb] + w1 * jnp.where(rank2_ref[tb, h, e2rows, :] < cnt,
                                                   e2w_ref[tb, h, e2rows, :], jnp.zeros((), BF16))

            def finish_part(eb):
                rows = slice(e1l * PEER_NKEYS + eb * E2_ROWS,
                             e1l * PEER_NKEYS + (eb + 1) * E2_ROWS)
                a = a_old[tb, rows, :]
                gelu = 0.5 * a * (1.0 + lax.erf(a * sqrt_half))
                lanes = slice((tb % 2) * LANES, (tb % 2 + 1) * LANES)
                act_new[tb // 2, rows, lanes] = gelu.astype(BF16) * w[eb]

            return ([functools.partial(head_part, h) for h in range(PEER_HEADS)]
                    + [functools.partial(finish_part, eb) for eb in range(nblk)])

        thunks = [t for e1l in range(PEER_E1) for tb in range(groups) for t in gate_thunks(e1l, tb)]

        halves = groups // 2
        kt_score, kt_value = d // MXU_DIM, ec // MXU_DIM
        score_rows, value_rows = min(SCORE_ROWS, ec), min(VALUE_ROWS, d)
        mxu_ops = [("score", rb, nb) for rb in range(ec // score_rows) for nb in range(halves)]
        mxu_ops += [("value", rb, nb) for rb in range(d // value_rows) for nb in range(halves)]
        n_small = halves * ((ec // MXU_DIM) * kt_score + (d // MXU_DIM) * kt_value)
        done_small, done_thunks = 0, 0

        def vpu_fill():
            nonlocal done_thunks
            target = -(-len(thunks) * done_small // n_small)
            while done_thunks < min(target, len(thunks)):
                thunks[done_thunks]()
                done_thunks += 1

        for kind, rb, nb in mxu_ops:
            tok = slice(nb * MXU_DIM, (nb + 1) * MXU_DIM)
            acc = None
            if kind == "score":
                rows = slice(rb * score_rows, (rb + 1) * score_rows)
                for k in range(kt_score):
                    ks = slice(k * MXU_DIM, (k + 1) * MXU_DIM)
                    part = _dot_t(u_ref[rows, ks], h2_ref[tok, ks])
                    acc = part if acc is None else acc + part
                    done_small += score_rows // MXU_DIM
                    vpu_fill()
                a_new[2 * nb, rows, :] = acc[:, :LANES]
                a_new[2 * nb + 1, rows, :] = acc[:, LANES:]
            else:
                rows = slice(rb * value_rows, (rb + 1) * value_rows)
                for k in range(kt_value):
                    ks = slice(k * MXU_DIM, (k + 1) * MXU_DIM)
                    part = jnp.dot(vt_ref[rows, ks], act_old[nb, ks, :], preferred_element_type=F32)
                    acc = part if acc is None else acc + part
                    done_small += value_rows // MXU_DIM
                    vpu_fill()
                yt_ref[rows, tok] += acc
        assert done_thunks == len(thunks)

    @pl.when(g % 2 == 0)
    def _():
        run(a0_scr, a1_scr, act1_scr, act0_scr)

    @pl.when(g % 2 == 1)
    def _():
        run(a1_scr, a0_scr, act0_scr, act1_scr)


def _peer(h2, u_b, vt_b, rank2, e2w, ce):
    n, d = h2.shape
    tp = min(PEER_ROWS, n)
    grp = tp // LANES
    ec = PEER_E1 * PEER_NKEYS
    nc = PEER_N // ec
    total = (n // tp) * nc
    lag = lambda g, k: jnp.clip(g - k, 0, total - 1)
    full = pl.BlockSpec((grp, PEER_HEADS, PEER_NKEYS, LANES), lambda g: (lag(g, 1) // nc, 0, 0, 0))
    part = pl.BlockSpec((grp, 2, PEER_HEADS, PEER_E1, LANES),
                        lambda g: (lag(g, 1) // nc, 0, 0, lag(g, 1) % nc, 0))
    return pl.pallas_call(
        _peer_kernel,
        grid=(total + 2,),
        in_specs=[pl.BlockSpec((tp, d), lambda g: (lag(g, 0) // nc, 0)),
                  pl.BlockSpec((ec, d), lambda g: (lag(g, 0) % nc, 0)),
                  pl.BlockSpec((None, d, ec), lambda g: (lag(g, 2) % nc, 0, 0)),
                  full, full, part],
        out_specs=pl.BlockSpec((d, tp), lambda g: (0, lag(g, 2) // nc)),
        out_shape=jax.ShapeDtypeStruct((d, n), F32),
        scratch_shapes=[pltpu.VMEM((grp, ec, LANES), F32), pltpu.VMEM((grp, ec, LANES), F32),
                        pltpu.VMEM((grp // 2, ec, MXU_DIM), BF16),
                        pltpu.VMEM((grp // 2, ec, MXU_DIM), BF16)],
        compiler_params=pltpu.CompilerParams(
            dimension_semantics=("arbitrary",), vmem_limit_bytes=VMEM_LIMIT),
        name="peer",
    )(h2, u_b, vt_b, rank2, e2w, ce)


def _final_kernel(x_ref, mod_ref, yt_ref, o_ref):
    gt2 = mod_ref[...][5:6]
    o_ref[...] = x_ref[...] + gt2 * yt_ref[...].T


def _final(x1, mod, yt):
    b, s, d = x1.shape
    tf = min(PEER_ROWS, s)
    per_seq = s // tf
    return pl.pallas_call(
        _final_kernel,
        grid=(b, per_seq),
        in_specs=[pl.BlockSpec((None, tf, d), lambda bi, si: (bi, si, 0)),
                  pl.BlockSpec((None, 6, d), lambda bi, si: (bi, 0, 0)),
                  pl.BlockSpec((d, tf), lambda bi, si: (0, bi * per_seq + si))],
        out_specs=pl.BlockSpec((None, tf, d), lambda bi, si: (bi, si, 0)),
        out_shape=jax.ShapeDtypeStruct((b, s, d), F32),
        compiler_params=pltpu.CompilerParams(
            dimension_semantics=("arbitrary", "arbitrary"), vmem_limit_bytes=VMEM_LIMIT),
        name="final",
    )(x1, mod, yt)


def _rope_tables(s):
    pos = jnp.arange(s, dtype=F32)
    inv_freq = ROPE_THETA ** (-jnp.arange(0, ROPE_DIM, 2, dtype=F32) / ROPE_DIM)
    ang = pos[:, None] * inv_freq[None, :]
    cos, sin = jnp.cos(ang), jnp.sin(ang)
    half = ROPE_DIM // 2
    ones = jnp.ones((s, HEAD_DIM - ROPE_DIM), F32)
    zeros = jnp.zeros((s, HEAD_DIM - ROPE_DIM), F32)
    zh = jnp.zeros((s, half), F32)
    rc = jnp.concatenate([cos, cos, ones], axis=1)
    rs1 = jnp.concatenate([-sin, zh, zeros], axis=1)
    rs2 = jnp.concatenate([zh, sin, zeros], axis=1)
    tile = lambda t: jnp.tile(t, (1, LANES // HEAD_DIM))
    return tile(rc), tile(rs1), tile(rs2)


def kernel(x, c, w_ada, b_ada, g_norm1, w_in, g_q, g_k, sinks, conv_w, g_out_attn,
           g_out_conv, w_out, g_norm2, w_pq, peer_subkeys, peer_u, peer_v):
    b, s, d = x.shape
    assert d == D_MODEL and s % WINDOW == 0 and (b * s) % LANES == 0

    mod = _ada(c, w_ada, b_ada).reshape(b, 6, d)

    rc, rs1, rs2 = _rope_tables(s)
    blk = jnp.arange(MXU_DIM) // HEAD_DIM
    bd = (blk[:, None] == blk[None, :]).astype(BF16)
    x1, h2 = _mixer(x, mod, g_norm1.reshape(1, d), w_in.astype(BF16),
                jnp.tile(g_q, N_Q_HEADS).reshape(1, ATTN_WIDTH),
                jnp.tile(g_k, N_KV_HEADS).reshape(1, KV_WIDTH),
                rc, rs1, rs2, conv_w, g_out_attn.reshape(1, ATTN_WIDTH),
                g_out_conv.reshape(1, CONV_WIDTH), w_out.astype(BF16), bd, sinks,
                g_norm2.reshape(1, d))

    a_mat = _fold(peer_subkeys.astype(BF16), w_pq.astype(BF16))
    h2 = h2.reshape(b * s, d)
    rank2, e2w, ce = _route(h2, a_mat, s)
    ec = PEER_E1 * PEER_NKEYS
    vt = peer_v.astype(BF16).reshape(PEER_N // ec, ec, d).transpose(0, 2, 1)
    yt = _peer(h2, peer_u.astype(BF16), vt, rank2, e2w, ce)
    return _final(x1, mod, yt)
```

```python
import functools
import math

import jax
import jax.numpy as jnp
from jax import lax
from jax.experimental import pallas as pl
from jax.experimental.pallas import tpu as pltpu

F32 = jnp.float32
BF16 = jnp.bfloat16

D_MODEL = 2048
HEAD_DIM = 64
N_Q_HEADS = 16
N_KV_HEADS = 4
Q_PER_KV = N_Q_HEADS // N_KV_HEADS
ATTN_WIDTH = N_Q_HEADS * HEAD_DIM
KV_WIDTH = N_KV_HEADS * HEAD_DIM
CONV_WIDTH = D_MODEL - ATTN_WIDTH
CONV_K = 3
IN_COLS = ATTN_WIDTH + 2 * KV_WIDTH + 3 * CONV_WIDTH
WINDOW = 128
ROPE_THETA = 500000.0
ROPE_DIM = HEAD_DIM // 4
ATTN_SCALE = HEAD_DIM ** -0.5
NEG_INF = -1e30
PEER_HEADS = 8
PEER_NKEYS = 128
PEER_N = PEER_NKEYS * PEER_NKEYS
PEER_HALF = 128
PEER_TOPK = 16
EPS = 1e-6

LANES = 128
MXU_DIM = 256
VMEM_LIMIT = 56 * 1024 * 1024

ADA_COLS = 1024
MIX_ROWS = 256
ROUTE_ROWS = 256
PEER_ROWS = 512
PEER_E1 = 8
E2_ROWS = 32
SCORE_ROWS = 512
VALUE_ROWS = 512

_PAIRS = tuple((a, b) for a in range(PEER_TOPK) for b in range(PEER_TOPK)
               if (a + 1) * (b + 1) <= PEER_TOPK)


def _dot_t(a, b):
    return lax.dot_general(a, b, (((1,), (1,)), ((), ())),
                           preferred_element_type=F32)


def _group_mean_sq(y, bd):
    y2 = y * y
    hi = y2.astype(BF16)
    lo = (y2 - hi.astype(F32)).astype(BF16)
    outs = []
    for j in range(y.shape[1] // MXU_DIM):
        sl = slice(MXU_DIM * j, MXU_DIM * (j + 1))
        outs.append(jnp.dot(hi[:, sl], bd, preferred_element_type=F32)
                    + jnp.dot(lo[:, sl], bd, preferred_element_type=F32))
    ss = outs[0] if len(outs) == 1 else jnp.concatenate(outs, axis=1)
    return ss * (1.0 / HEAD_DIM)


def _rope(t, rc, rs1, rs2):
    outs = []
    for j in range(t.shape[1] // LANES):
        tj = t[:, LANES * j:LANES * (j + 1)]
        outs.append(tj * rc
                    + pltpu.roll(tj, LANES - ROPE_DIM // 2, 1) * rs1
                    + pltpu.roll(tj, ROPE_DIM // 2, 1) * rs2)
    return outs[0] if len(outs) == 1 else jnp.concatenate(outs, axis=1)


def _ada_kernel(c_ref, w_ref, b_ref, o_ref):
    c = c_ref[...]
    s = c * jax.nn.sigmoid(c)
    o_ref[...] = jnp.dot(s, w_ref[...], preferred_element_type=F32,
                         precision=lax.Precision.HIGHEST) + b_ref[...]


def _ada(c, w_ada, b_ada):
    b, d = c.shape
    n = w_ada.shape[1]
    return pl.pallas_call(
        _ada_kernel,
        grid=(n // ADA_COLS,),
        in_specs=[pl.BlockSpec((b, d), lambda j: (0, 0)),
                  pl.BlockSpec((d, ADA_COLS), lambda j: (0, j)),
                  pl.BlockSpec((1, ADA_COLS), lambda j: (0, j))],
        out_specs=pl.BlockSpec((b, ADA_COLS), lambda j: (0, j)),
        out_shape=jax.ShapeDtypeStruct((b, n), F32),
        compiler_params=pltpu.CompilerParams(
            dimension_semantics=("arbitrary",), vmem_limit_bytes=VMEM_LIMIT),
        name="ada",
    )(c, w_ada, b_ada.reshape(1, n))


def _mixer_kernel(sinks_ref, x_ref, mod_ref, g1_ref, win_ref, gq_ref, gk_ref,
                  rc_ref, rs1_ref, rs2_ref, cw_ref, goa_ref, goc_ref, wout_ref,
                  bd_ref, g2_ref, o_ref, h2_ref, kk_ref, vv_ref, u_ref, qn_ref, mixed_ref):
    ts = x_ref.shape[0]
    si = pl.program_id(1)
    x = x_ref[...]
    mod = mod_ref[...]
    sh1, sc1, gt1 = mod[0:1], mod[1:2], mod[2:3]
    bd = bd_ref[...]
    rc, rs1, rs2 = rc_ref[...], rs1_ref[...], rs2_ref[...]

    @pl.when(si == 0)
    def _():
        kk_ref[:, :, 0:WINDOW, :] = jnp.zeros((N_KV_HEADS, 2, WINDOW, LANES), BF16)
        vv_ref[:, :, 0:WINDOW, :] = jnp.zeros((N_KV_HEADS, 2, WINDOW, LANES), BF16)
        u_ref[0:8, :] = jnp.zeros((8, CONV_WIDTH), F32)

    @pl.when(si > 0)
    def _():
        kk_ref[:, :, 0:WINDOW, :] = kk_ref[:, :, ts:ts + WINDOW, :]
        vv_ref[:, :, 0:WINDOW, :] = vv_ref[:, :, ts:ts + WINDOW, :]
        u_ref[0:8, :] = u_ref[ts:ts + 8, :]

    ms = jnp.mean(x * x, axis=-1, keepdims=True)
    h1 = ((x * lax.rsqrt(ms + EPS)) * g1_ref[...]) * (1.0 + sc1) + sh1
    h1b = h1.astype(BF16)

    def proj(lo, width):
        return jnp.dot(h1b, win_ref[:, lo:lo + width], preferred_element_type=F32)

    q = proj(0, ATTN_WIDTH)
    q = q * lax.rsqrt(_group_mean_sq(q, bd) + EPS) * gq_ref[...]
    qn_ref[...] = (_rope(q, rc, rs1, rs2) * ATTN_SCALE).astype(BF16)

    k = proj(ATTN_WIDTH, KV_WIDTH)
    k = k * lax.rsqrt(_group_mean_sq(k, bd) + EPS) * gk_ref[...]
    k = _rope(k, rc, rs1, rs2)
    v = proj(ATTN_WIDTH + KV_WIDTH, KV_WIDTH)
    low = lax.broadcasted_iota(jnp.int32, (1, LANES), 1) < HEAD_DIM
    for g in range(N_KV_HEADS):
        col = slice(LANES * (g // 2), LANES * (g // 2 + 1))
        for t, dst in ((k[:, col], kk_ref), (v[:, col], vv_ref)):
            tr = pltpu.roll(t, HEAD_DIM, 1)
            in_low, in_high = (t, tr) if g % 2 == 0 else (tr, t)
            dst[g, 0, WINDOW:WINDOW + ts, :] = jnp.where(low, in_low, 0.0).astype(BF16)
            dst[g, 1, WINDOW:WINDOW + ts, :] = jnp.where(low, 0.0, in_high).astype(BF16)

    qi = lax.broadcasted_iota(jnp.int32, (WINDOW, 2 * WINDOW), 0)
    kj = lax.broadcasted_iota(jnp.int32, (WINDOW, 2 * WINDOW), 1)
    band = (kj > qi) & (kj <= qi + WINDOW)

    def scores(jb, g):
        qrows = slice(jb * WINDOW, (jb + 1) * WINDOW)
        krows = slice(jb * WINDOW, (jb + 2) * WINDOW)
        qa = qn_ref[qrows, MXU_DIM * g:MXU_DIM * g + LANES]
        qb = qn_ref[qrows, MXU_DIM * g + LANES:MXU_DIM * (g + 1)]
        q2 = jnp.concatenate([qa, qb], axis=0)
        return (_dot_t(q2, kk_ref[g, 0, krows, :]),
                _dot_t(q2, kk_ref[g, 1, krows, :]))

    def attend(jb, g, s_lo, s_hi):
        first_key = jnp.where(si * (ts // WINDOW) + jb == 0, WINDOW, 0)
        mask = band & (kj >= first_key)
        qrows = slice(jb * WINDOW, (jb + 1) * WINDOW)
        krows = slice(jb * WINDOW, (jb + 2) * WINDOW)
        vcat = jnp.concatenate([vv_ref[g, 0, krows, :], vv_ref[g, 1, krows, :]], axis=0)
        halves = []
        for half in range(2):
            rows = slice(half * WINDOW, (half + 1) * WINDOW)
            ps, inv = [], []
            for o, s in enumerate((s_lo[rows], s_hi[rows])):
                sink = sinks_ref[Q_PER_KV * g + 2 * half + o]
                s = jnp.where(mask, s, NEG_INF)
                m = jnp.maximum(jnp.max(s, axis=-1, keepdims=True), sink)
                p = jnp.exp(s - m)
                denom = jnp.sum(p, axis=-1, keepdims=True) + jnp.exp(sink - m)
                ps.append(p.astype(BF16))
                inv.append(1.0 / denom)
            pv = jnp.dot(jnp.concatenate(ps, axis=1), vcat, preferred_element_type=F32)
            halves.append(pv * jnp.where(low, inv[0], inv[1]))
        att = jnp.concatenate(halves, axis=1)
        cols = slice(MXU_DIM * g, MXU_DIM * (g + 1))
        att = att * lax.rsqrt(_group_mean_sq(att, bd) + EPS) * goa_ref[:, cols]
        mixed_ref[qrows, cols] = att.astype(BF16)

    o0 = ATTN_WIDTH + 2 * KV_WIDTH
    cw = cw_ref[...]

    def conv_block(cb):
        cols = slice(cb * MXU_DIM, (cb + 1) * MXU_DIM)
        bg = proj(o0 + cb * MXU_DIM, MXU_DIM)
        u_ref[8:8 + ts, cols] = (proj(o0 + CONV_WIDTH + cb * MXU_DIM, MXU_DIM)
                                 * proj(o0 + 2 * CONV_WIDTH + cb * MXU_DIM, MXU_DIM))
        acc = cw[0:1, cols] * u_ref[6:6 + ts, cols]
        acc = acc + cw[1:2, cols] * u_ref[7:7 + ts, cols]
        acc = acc + cw[2:3, cols] * u_ref[8:8 + ts, cols]
        y = bg * acc
        y = y * lax.rsqrt(_group_mean_sq(y, bd) + EPS) * goc_ref[:, cols]
        mixed_ref[:, ATTN_WIDTH + cb * MXU_DIM:ATTN_WIDTH + (cb + 1) * MXU_DIM] = y.astype(BF16)

    iters = [(jb, g) for jb in range(ts // WINDOW) for g in range(N_KV_HEADS)]
    n_conv = CONV_WIDTH // MXU_DIM
    conv_done = 0
    s_cur = scores(*iters[0])
    for i, (jb, g) in enumerate(iters):
        s_next = scores(*iters[i + 1]) if i + 1 < len(iters) else None
        want = -(-n_conv * (i + 1) // len(iters))
        while conv_done < want:
            conv_block(conv_done)
            conv_done += 1
        attend(jb, g, *s_cur)
        s_cur = s_next

    d_model = x_ref.shape[1]
    col_blocks = [slice(c, c + 2 * MXU_DIM) for c in range(0, d_model, 2 * MXU_DIM)]
    ss2 = jnp.zeros((ts, 1), F32)
    for cols in col_blocks:
        x1c = x_ref[:, cols] + gt1[:, cols] * jnp.dot(mixed_ref[...], wout_ref[:, cols],
                                                      preferred_element_type=F32)
        o_ref[:, cols] = x1c
        ss2 = ss2 + jnp.sum(x1c * x1c, axis=-1, keepdims=True)

    sh2, sc2 = mod[3:4], mod[4:5]
    rs2 = lax.rsqrt(ss2 * (1.0 / d_model) + EPS)
    g2 = g2_ref[...]
    for cols in col_blocks:
        h2c = ((o_ref[:, cols] * rs2) * g2[:, cols]) * (1.0 + sc2[:, cols]) + sh2[:, cols]
        h2_ref[:, cols] = h2c.astype(BF16)


def _mixer(x, mod, g_norm1, w_in_b, gq, gk, rc, rs1, rs2, conv_w, goa, goc, w_out_b, bd, sinks,
           g_norm2):
    b, s, d = x.shape
    ts = min(MIX_ROWS, s)
    const = lambda shape: pl.BlockSpec(shape, lambda bi, si: (0,) * len(shape),
                                       pipeline_mode=pl.Buffered(1))
    rope_spec = pl.BlockSpec((ts, LANES), lambda bi, si: (si, 0))
    return pl.pallas_call(
        _mixer_kernel,
        grid=(b, s // ts),
        in_specs=[pl.BlockSpec(memory_space=pltpu.SMEM),
                  pl.BlockSpec((None, ts, d), lambda bi, si: (bi, si, 0)),
                  pl.BlockSpec((None, 6, d), lambda bi, si: (bi, 0, 0)),
                  const((1, d)),
                  const((d, IN_COLS)),
                  const((1, ATTN_WIDTH)),
                  const((1, KV_WIDTH)),
                  rope_spec, rope_spec, rope_spec,
                  const((CONV_K, CONV_WIDTH)),
                  const((1, ATTN_WIDTH)),
                  const((1, CONV_WIDTH)),
                  const((d, d)),
                  const((MXU_DIM, MXU_DIM)),
                  const((1, d))],
        out_specs=[pl.BlockSpec((None, ts, d), lambda bi, si: (bi, si, 0)),
                   pl.BlockSpec((None, ts, d), lambda bi, si: (bi, si, 0))],
        out_shape=[jax.ShapeDtypeStruct((b, s, d), F32), jax.ShapeDtypeStruct((b, s, d), BF16)],
        scratch_shapes=[pltpu.VMEM((N_KV_HEADS, 2, ts + WINDOW, LANES), BF16),
                        pltpu.VMEM((N_KV_HEADS, 2, ts + WINDOW, LANES), BF16),
                        pltpu.VMEM((ts + 8, CONV_WIDTH), F32),
                        pltpu.VMEM((ts, ATTN_WIDTH), BF16),
                        pltpu.VMEM((ts, d), BF16)],
        compiler_params=pltpu.CompilerParams(
            dimension_semantics=("arbitrary", "arbitrary"), vmem_limit_bytes=VMEM_LIMIT),
        name="mixer",
    )(sinks, x, mod, g_norm1, w_in_b, gq, gk, rc, rs1, rs2, conv_w, goa, goc, w_out_b, bd, g_norm2)


def _fold_kernel(sk_ref, w_ref, o_ref):
    o_ref[...] = _dot_t(sk_ref[...], w_ref[...]).astype(BF16)


def _fold(subkeys_b, w_pq_b):
    d = w_pq_b.shape[0]
    return pl.pallas_call(
        _fold_kernel,
        grid=(2, PEER_HEADS),
        in_specs=[pl.BlockSpec((None, None, PEER_NKEYS, PEER_HALF), lambda p, h: (h, p, 0, 0)),
                  pl.BlockSpec((d, PEER_HALF), lambda p, h: (0, 2 * h + p))],
        out_specs=pl.BlockSpec((PEER_NKEYS, d), lambda p, h: (p * PEER_HEADS + h, 0)),
        out_shape=jax.ShapeDtypeStruct((2 * PEER_HEADS * PEER_NKEYS, d), BF16),
        compiler_params=pltpu.CompilerParams(
            dimension_semantics=("arbitrary", "arbitrary"), vmem_limit_bytes=VMEM_LIMIT),
        name="fold",
    )(subkeys_b, w_pq_b)


def _compare_exchange(v, i, l, descending):
    hi, lo = jnp.maximum(v[i], v[l]), jnp.minimum(v[i], v[l])
    v[i], v[l] = (hi, lo) if descending else (lo, hi)


def _bitonic_merge(v):
    n = len(v)
    j = n // 2
    while j >= 1:
        for i in range(n):
            if i ^ j > i:
                _compare_exchange(v, i, i ^ j, True)
        j //= 2
    return v


def _top16(s, want_rank=False):
    n = PEER_TOPK
    sub = s.shape[0] // n
    v = [s[sub * j:sub * (j + 1), :] for j in range(n)]
    k = 2
    while k <= n:
        j = k // 2
        while j >= 1:
            for i in range(n):
                if i ^ j > i:
                    _compare_exchange(v, i, i ^ j, (i & k) == 0)
            j //= 2
        k *= 2
    shift = sub // 2
    while shift >= 1:
        rolled = [pltpu.roll(w, shift, 0) for w in v]
        v = _bitonic_merge([jnp.maximum(v[i], rolled[n - 1 - i]) for i in range(n)])
        shift //= 2
    t = [w[0:1, :] for w in v]
    if not want_rank:
        return t
    b8 = t[7] > s
    b4 = jnp.where(b8, t[11], t[3]) > s
    b2 = jnp.where(b8, jnp.where(b4, t[13], t[9]), jnp.where(b4, t[5], t[1])) > s
    b1 = jnp.where(b8,
                   jnp.where(b4, jnp.where(b2, t[14], t[12]), jnp.where(b2, t[10], t[8])),
                   jnp.where(b4, jnp.where(b2, t[6], t[4]), jnp.where(b2, t[2], t[0]))) > s
    rank = (jnp.where(b8, 8.0, 0.0) + jnp.where(b4, 4.0, 0.0) + jnp.where(b2, 2.0, 0.0)
            + jnp.where(b1, 1.0, 0.0) + jnp.where(t[15] > s, 1.0, 0.0))
    return t, rank


def _route_kernel(h2_ref, a_ref, rank2_ref, e2w_ref, cnt_ref, e1w_ref, st0_ref, st1_ref):
    i = pl.program_id(0)

    @pl.when(i == 0)
    def _():
        st1_ref[...] = jnp.zeros(st1_ref.shape, F32)

    hk = PEER_HEADS * PEER_NKEYS
    tr = h2_ref.shape[0]

    def run(st_new, st_old):
        thunks = []
        for g in range(tr // LANES):
            lanes = slice(g * LANES, (g + 1) * LANES)
            t1, t2 = [None] * PEER_HEADS, [None] * PEER_HEADS
            state = {}

            def top1(h, lanes=lanes, t1=t1):
                t1[h] = _top16(st_old[h * PEER_NKEYS:(h + 1) * PEER_NKEYS, lanes])

            def top2(h, g=g, lanes=lanes, t2=t2):
                vals, rank = _top16(st_old[hk + h * PEER_NKEYS:hk + (h + 1) * PEER_NKEYS, lanes], True)
                t2[h] = vals
                rank2_ref[g, h] = rank.astype(BF16)

            def candidates(t1=t1, t2=t2, state=state):
                r1 = [jnp.concatenate([t1[h][a] for h in range(PEER_HEADS)], axis=0)
                      for a in range(PEER_TOPK)]
                r2 = [jnp.concatenate([t2[h][a] for h in range(PEER_HEADS)], axis=0)
                      for a in range(PEER_TOPK)]
                zs = [r1[a] + r2[b] for (a, b) in _PAIRS]
                cur = list(zs)
                z16 = None
                for it in range(PEER_TOPK):
                    z16 = functools.reduce(jnp.maximum, cur)
                    if it + 1 < PEER_TOPK:
                        cur = [jnp.where(c == z16, -jnp.inf, c) for c in cur]
                zmax = zs[0]
                zsum = jnp.zeros_like(zmax)
                cnt = [jnp.zeros_like(zmax) for _ in range(PEER_TOPK)]
                for (a, b), z in zip(_PAIRS, zs):
                    sel = z >= z16
                    zsum = zsum + jnp.where(sel, jnp.exp(z - zmax), 0.0)
                    cnt[a] = cnt[a] + jnp.where(sel, 1.0, 0.0)
                state["cnt"] = cnt
                state["inv_z"] = 1.0 / zsum

            def finish(h, g=g, lanes=lanes, t1=t1, t2=t2, state=state):
                s1 = st_old[h * PEER_NKEYS:(h + 1) * PEER_NKEYS, lanes]
                s2 = st_old[hk + h * PEER_NKEYS:hk + (h + 1) * PEER_NKEYS, lanes]
                cnt_full = jnp.zeros_like(s1)
                for a in range(PEER_TOPK):
                    cnt_full = jnp.where(s1 == t1[h][a], state["cnt"][a][h:h + 1, :], cnt_full)
                e2w_ref[g, h] = (jnp.exp(s2 - t2[h][0]) * state["inv_z"][h:h + 1, :]).astype(BF16)
                cnt_ref[g, h] = cnt_full
                e1w_ref[g, h] = jnp.exp(s1 - t1[h][0])

            for h in range(PEER_HEADS):
                thunks.append(functools.partial(top1, h))
                thunks.append(functools.partial(top2, h))
            thunks.append(candidates)
            for h in range(PEER_HEADS):
                thunks.append(functools.partial(finish, h))

        rows_per = 512
        n_rb, n_k = st_new.shape[0] // rows_per, h2_ref.shape[1] // MXU_DIM
        n_dots, done_dots, done_thunks = n_rb * n_k, 0, 0
        for rb in range(n_rb):
            rows = slice(rb * rows_per, (rb + 1) * rows_per)
            acc = None
            for k in range(n_k):
                ks = slice(k * MXU_DIM, (k + 1) * MXU_DIM)
                part = _dot_t(a_ref[rows, ks], h2_ref[:, ks])
                acc = part if acc is None else acc + part
                done_dots += 1
                target = -(-len(thunks) * done_dots // n_dots)
                while done_thunks < target:
                    thunks[done_thunks]()
                    done_thunks += 1
            st_new[rows, :] = acc

    @pl.when(i % 2 == 0)
    def _():
        run(st0_ref, st1_ref)

    @pl.when(i % 2 == 1)
    def _():
        run(st1_ref, st0_ref)


def _route(h2, a_mat, seq):
    n, d = h2.shape
    tr = min(ROUTE_ROWS, seq)
    grp = tr // LANES
    tiles = n // tr
    cur = lambda i: jnp.minimum(i, tiles - 1)
    prev = lambda i: jnp.maximum(i - 1, 0)
    rshape = (n // LANES, PEER_HEADS, PEER_NKEYS, LANES)
    rspec = pl.BlockSpec((grp, PEER_HEADS, PEER_NKEYS, LANES), lambda i: (prev(i), 0, 0, 0))
    return pl.pallas_call(
        _route_kernel,
        grid=(tiles + 1,),
        in_specs=[pl.BlockSpec((tr, d), lambda i: (cur(i), 0)),
                  pl.BlockSpec(a_mat.shape, lambda i: (0, 0), pipeline_mode=pl.Buffered(1))],
        out_specs=[rspec, rspec, rspec, rspec],
        out_shape=[jax.ShapeDtypeStruct(rshape, BF16), jax.ShapeDtypeStruct(rshape, BF16),
                   jax.ShapeDtypeStruct(rshape, F32), jax.ShapeDtypeStruct(rshape, F32)],
        scratch_shapes=[pltpu.VMEM((a_mat.shape[0], tr), F32), pltpu.VMEM((a_mat.shape[0], tr), F32)],
        compiler_params=pltpu.CompilerParams(
            dimension_semantics=("arbitrary",), vmem_limit_bytes=VMEM_LIMIT),
        name="route",
    )(h2, a_mat)


def _peer_kernel(h2_ref, u_ref, vt_ref, rank2_ref, e2w_ref, cnt_ref, e1w_ref, yt_ref,
                 a0_scr, a1_scr, act0_scr, act1_scr):
    g = pl.program_id(0)
    groups, ec = a0_scr.shape[0], a0_scr.shape[1]
    d = yt_ref.shape[0]
    n_chunks = PEER_NKEYS // PEER_E1

    @pl.when(g == 0)
    def _():
        for ref in (a0_scr, a1_scr, act0_scr, act1_scr):
            ref[...] = jnp.zeros(ref.shape, ref.dtype)

    @pl.when((g == 0) | ((g >= 2) & ((g - 2) % n_chunks == 0)))
    def _():
        yt_ref[...] = jnp.zeros(yt_ref.shape, F32)

    sqrt_half = math.sqrt(0.5)

    def run(a_new, a_old, act_new, act_old):
        nblk = PEER_NKEYS // E2_ROWS

        def gate_thunks(e1l, tb):
            w = [jnp.zeros((E2_ROWS, LANES), BF16) for _ in range(nblk)]

            def head_part(h):
                cnt = jnp.broadcast_to(cnt_ref[tb, h, e1l:e1l + 1, :], (E2_ROWS, LANES)).astype(BF16)
                w1 = jnp.broadcast_to(e1w_ref[tb, h, e1l:e1l + 1, :], (E2_ROWS, LANES)).astype(BF16)
                for eb in range(nblk):
                    e2rows = slice(eb * E2_ROWS, (eb + 1) * E2_ROWS)
                    w[eb] = w[eb] + w1 * jnp.where(rank2_ref[tb, h, e2rows, :] < cnt,
                                                   e2w_ref[tb, h, e2rows, :], jnp.zeros((), BF16))

            def finish_part(eb):
                rows = slice(e1l * PEER_NKEYS + eb * E2_ROWS,
                             e1l * PEER_NKEYS + (eb + 1) * E2_ROWS)
                a = a_old[tb, rows, :]
                gelu = 0.5 * a * (1.0 + lax.erf(a * sqrt_half))
                lanes = slice((tb % 2) * LANES, (tb % 2 + 1) * LANES)
                act_new[tb // 2, rows, lanes] = gelu.astype(BF16) * w[eb]

            return ([functools.partial(head_part, h) for h in range(PEER_HEADS)]
                    + [functools.partial(finish_part, eb) for eb in range(nblk)])

        thunks = [t for e1l in range(PEER_E1) for tb in range(groups) for t in gate_thunks(e1l, tb)]

        halves = groups // 2
        kt_score, kt_value = d // MXU_DIM, ec // MXU_DIM
        score_rows, value_rows = min(SCORE_ROWS, ec), min(VALUE_ROWS, d)
        mxu_ops = [("score", rb, nb) for rb in range(ec // score_rows) for nb in range(halves)]
        mxu_ops += [("value", rb, nb) for rb in range(d // value_rows) for nb in range(halves)]
        n_small = halves * ((ec // MXU_DIM) * kt_score + (d // MXU_DIM) * kt_value)
        done_small, done_thunks = 0, 0

        def vpu_fill():
            nonlocal done_thunks
            target = -(-len(thunks) * done_small // n_small)
            while done_thunks < min(target, len(thunks)):
                thunks[done_thunks]()
                done_thunks += 1

        for kind, rb, nb in mxu_ops:
            tok = slice(nb * MXU_DIM, (nb + 1) * MXU_DIM)
            acc = None
            if kind == "score":
                rows = slice(rb * score_rows, (rb + 1) * score_rows)
                for k in range(kt_score):
                    ks = slice(k * MXU_DIM, (k + 1) * MXU_DIM)
                    part = _dot_t(u_ref[rows, ks], h2_ref[tok, ks])
                    acc = part if acc is None else acc + part
                    done_small += score_rows // MXU_DIM
                    vpu_fill()
                a_new[2 * nb, rows, :] = acc[:, :LANES]
                a_new[2 * nb + 1, rows, :] = acc[:, LANES:]
            else:
                rows = slice(rb * value_rows, (rb + 1) * value_rows)
                for k in range(kt_value):
                    ks = slice(k * MXU_DIM, (k + 1) * MXU_DIM)
                    part = jnp.dot(vt_ref[rows, ks], act_old[nb, ks, :], preferred_element_type=F32)
                    acc = part if acc is None else acc + part
                    done_small += value_rows // MXU_DIM
                    vpu_fill()
                yt_ref[rows, tok] += acc
        assert done_thunks == len(thunks)

    @pl.when(g % 2 == 0)
    def _():
        run(a0_scr, a1_scr, act1_scr, act0_scr)

    @pl.when(g % 2 == 1)
    def _():
        run(a1_scr, a0_scr, act0_scr, act1_scr)


def _peer(h2, u_b, vt_b, rank2, e2w, cnt, e1w):
    n, d = h2.shape
    tp = min(PEER_ROWS, n)
    grp = tp // LANES
    ec = PEER_E1 * PEER_NKEYS
    nc = PEER_N // ec
    total = (n // tp) * nc
    lag = lambda g, k: jnp.clip(g - k, 0, total - 1)
    full = pl.BlockSpec((grp, PEER_HEADS, PEER_NKEYS, LANES), lambda g: (lag(g, 1) // nc, 0, 0, 0))
    part = pl.BlockSpec((grp, PEER_HEADS, PEER_E1, LANES),
                        lambda g: (lag(g, 1) // nc, 0, lag(g, 1) % nc, 0))
    return pl.pallas_call(
        _peer_kernel,
        grid=(total + 2,),
        in_specs=[pl.BlockSpec((tp, d), lambda g: (lag(g, 0) // nc, 0)),
                  pl.BlockSpec((ec, d), lambda g: (lag(g, 0) % nc, 0)),
                  pl.BlockSpec((None, d, ec), lambda g: (lag(g, 2) % nc, 0, 0)),
                  full, full, part, part],
        out_specs=pl.BlockSpec((d, tp), lambda g: (0, lag(g, 2) // nc)),
        out_shape=jax.ShapeDtypeStruct((d, n), F32),
        scratch_shapes=[pltpu.VMEM((grp, ec, LANES), F32), pltpu.VMEM((grp, ec, LANES), F32),
                        pltpu.VMEM((grp // 2, ec, MXU_DIM), BF16),
                        pltpu.VMEM((grp // 2, ec, MXU_DIM), BF16)],
        compiler_params=pltpu.CompilerParams(
            dimension_semantics=("arbitrary",), vmem_limit_bytes=VMEM_LIMIT),
        name="peer",
    )(h2, u_b, vt_b, rank2, e2w, cnt, e1w)


def _final_kernel(x_ref, mod_ref, yt_ref, o_ref):
    gt2 = mod_ref[...][5:6]
    o_ref[...] = x_ref[...] + gt2 * yt_ref[...].T


def _final(x1, mod, yt):
    b, s, d = x1.shape
    tf = min(PEER_ROWS, s)
    per_seq = s // tf
    return pl.pallas_call(
        _final_kernel,
        grid=(b, per_seq),
        in_specs=[pl.BlockSpec((None, tf, d), lambda bi, si: (bi, si, 0)),
                  pl.BlockSpec((None, 6, d), lambda bi, si: (bi, 0, 0)),
                  pl.BlockSpec((d, tf), lambda bi, si: (0, bi * per_seq + si))],
        out_specs=pl.BlockSpec((None, tf, d), lambda bi, si: (bi, si, 0)),
        out_shape=jax.ShapeDtypeStruct((b, s, d), F32),
        compiler_params=pltpu.CompilerParams(
            dimension_semantics=("arbitrary", "arbitrary"), vmem_limit_bytes=VMEM_LIMIT),
        name="final",
    )(x1, mod, yt)


def _rope_tables(s):
    pos = jnp.arange(s, dtype=F32)
    inv_freq = ROPE_THETA ** (-jnp.arange(0, ROPE_DIM, 2, dtype=F32) / ROPE_DIM)
    ang = pos[:, None] * inv_freq[None, :]
    cos, sin = jnp.cos(ang), jnp.sin(ang)
    half = ROPE_DIM // 2
    ones = jnp.ones((s, HEAD_DIM - ROPE_DIM), F32)
    zeros = jnp.zeros((s, HEAD_DIM - ROPE_DIM), F32)
    zh = jnp.zeros((s, half), F32)
    rc = jnp.concatenate([cos, cos, ones], axis=1)
    rs1 = jnp.concatenate([-sin, zh, zeros], axis=1)
    rs2 = jnp.concatenate([zh, sin, zeros], axis=1)
    tile = lambda t: jnp.tile(t, (1, LANES // HEAD_DIM))
    return tile(rc), tile(rs1), tile(rs2)


def kernel(x, c, w_ada, b_ada, g_norm1, w_in, g_q, g_k, sinks, conv_w, g_out_attn,
           g_out_conv, w_out, g_norm2, w_pq, peer_subkeys, peer_u, peer_v):
    b, s, d = x.shape
    assert d == D_MODEL and s % WINDOW == 0 and (b * s) % LANES == 0

    mod = _ada(c, w_ada, b_ada).reshape(b, 6, d)

    rc, rs1, rs2 = _rope_tables(s)
    blk = jnp.arange(MXU_DIM) // HEAD_DIM
    bd = (blk[:, None] == blk[None, :]).astype(BF16)
    x1, h2 = _mixer(x, mod, g_norm1.reshape(1, d), w_in.astype(BF16),
                jnp.tile(g_q, N_Q_HEADS).reshape(1, ATTN_WIDTH),
                jnp.tile(g_k, N_KV_HEADS).reshape(1, KV_WIDTH),
                rc, rs1, rs2, conv_w, g_out_attn.reshape(1, ATTN_WIDTH),
                g_out_conv.reshape(1, CONV_WIDTH), w_out.astype(BF16), bd, sinks,
                g_norm2.reshape(1, d))

    a_mat = _fold(peer_subkeys.astype(BF16), w_pq.astype(BF16))
    h2 = h2.reshape(b * s, d)
    rank2, e2w, cnt, e1w = _route(h2, a_mat, s)
    ec = PEER_E1 * PEER_NKEYS
    vt = peer_v.astype(BF16).reshape(PEER_N // ec, ec, d).transpose(0, 2, 1)
    yt = _peer(h2, peer_u.astype(BF16), vt, rank2, e2w, cnt, e1w)
    return _final(x1, mod, yt)
```
